```python
import math
import jax, jax.numpy as jnp
from jax import lax
import numpy as np

D_MODEL = 1024
BATCH = 16
SEQ = 4096
DEPTH = 2

CTX_LEN = 256
GRID_W = 64
N_DIFF_HEADS = 4
DIFF_HEAD_DIM = 64
DIFF_V_DIM = 2 * DIFF_HEAD_DIM
DIFF_WIDTH = N_DIFF_HEADS * DIFF_V_DIM
ROPE_FREQS = DIFF_HEAD_DIM // 4
ROPE_THETA = 10000.0
Q_BLOCK = 128
POOL_WINDOWS = (2, 4, 8, 16)
N_POOL_GROUPS = len(POOL_WINDOWS)
POOL_GROUP_DIM = D_MODEL // 8
POOL_WIDTH = N_POOL_GROUPS * POOL_GROUP_DIM
MIX_WIDTH = DIFF_WIDTH + POOL_WIDTH
MIX_IN_WIDTH = 3 * DIFF_WIDTH + POOL_WIDTH
CONV_WIDTH = 31
CONV_CH = D_MODEL
D_FF = 2816
N_MOD = 9
N_EVEN = (DEPTH + 1) // 2
N_ODD = DEPTH // 2
ALPHA = (2.0 * DEPTH) ** 0.25
BETA = (8.0 * DEPTH) ** -0.25
FFN_HALF = 0.5
LN_EPS = 1e-5

kernel_name = 'hybrid_diffattn_pool_conformer_dit'


def layer_norm(x, g, b):
    xf = x.astype(jnp.float32)
    mu = jnp.mean(xf, axis=-1, keepdims=True)
    var = jnp.mean(jnp.square(xf - mu), axis=-1, keepdims=True)
    y = (xf - mu) * lax.rsqrt(var + LN_EPS)
    return (y * g.astype(jnp.float32) + b.astype(jnp.float32)).astype(x.dtype)


def rms_norm(x, g):
    xf = x.astype(jnp.float32)
    y = xf * lax.rsqrt(jnp.mean(jnp.square(xf), axis=-1, keepdims=True) + LN_EPS)
    return (y * g.astype(jnp.float32)).astype(x.dtype)


def modulate(x, shift, scale):
    return x * (1.0 + scale) + shift


def swiglu(h, w_in, w_out):
    gate, up = jnp.split(h @ w_in, 2, axis=-1)
    return (jax.nn.silu(gate) * up) @ w_out


def axial_rope_tables(rows):
    row = jnp.repeat(jnp.arange(rows, dtype=jnp.float32), GRID_W)
    col = jnp.tile(jnp.arange(GRID_W, dtype=jnp.float32), rows)
    inv_freq = ROPE_THETA ** (-jnp.arange(ROPE_FREQS, dtype=jnp.float32) / ROPE_FREQS)
    ang = jnp.stack([row[:, None] * inv_freq, col[:, None] * inv_freq], axis=1)
    return jnp.cos(ang), jnp.sin(ang)


def apply_axial_rope(x, cos, sin):
    xs = x.reshape(x.shape[:-1] + (2, 2, ROPE_FREQS))
    x1, x2 = xs[..., 0, :], xs[..., 1, :]
    c = cos[:, None, None].astype(x.dtype)
    s = sin[:, None, None].astype(x.dtype)
    out = jnp.stack([x1 * c - x2 * s, x2 * c + x1 * s], axis=-2)
    return out.reshape(x.shape)


def diff_attention(q, k, v, lam):
    b, t = q.shape[:2]
    n_blk = t // Q_BLOCK
    scale = DIFF_HEAD_DIM ** -0.5
    q_blocks = jnp.moveaxis(q.reshape((b, n_blk, Q_BLOCK) + q.shape[2:]), 1, 0)

    def one_block(qb):
        s = jnp.einsum('bqhmd,bkhmd->bhmqk', qb, k).astype(jnp.float32) * scale
        p = jax.nn.softmax(s, axis=-1)
        p_diff = p[:, :, 0] - lam * p[:, :, 1]
        return jnp.einsum('bhqk,bkhe->bqhe', p_diff.astype(v.dtype), v)

    out = lax.map(one_block, q_blocks)
    return jnp.moveaxis(out, 0, 1).reshape(b, t, N_DIFF_HEADS, DIFF_V_DIM)


def multiscale_pool(u, w_pool, pool_scale):
    b, t, _ = u.shape
    uf = u.astype(jnp.float32)
    csum = jnp.concatenate([jnp.zeros((b, 1, POOL_WIDTH), jnp.float32), jnp.cumsum(uf, axis=1)], axis=1)
    pos = jnp.arange(t)
    groups = []
    for g, win in enumerate(POOL_WINDOWS):
        lo = jnp.clip(pos - win // 2, 0, t)
        hi = jnp.clip(pos - win // 2 + win, 0, t)
        sl = slice(g * POOL_GROUP_DIM, (g + 1) * POOL_GROUP_DIM)
        cs = csum[..., sl]
        cnt = (hi - lo).astype(jnp.float32)[None, :, None]
        mean = (jnp.take(cs, hi, axis=1) - jnp.take(cs, lo, axis=1)) / cnt
        groups.append(mean - uf[..., sl])
    pooled = jnp.stack(groups, axis=2).astype(u.dtype)
    y = jnp.einsum('btgc,gcd->btgd', pooled, w_pool).reshape(b, t, POOL_WIDTH)
    return y * pool_scale


def conv_module(h, w_c1, b_c1, w_dw, b_dw, ln_g, ln_b, w_c2, b_c2):
    a, g = jnp.split(h @ w_c1 + b_c1, 2, axis=-1)
    z = a * jax.nn.sigmoid(g)
    z = lax.conv_general_dilated(
        z, w_dw[:, None, :], window_strides=(1,),
        padding=[(CONV_WIDTH // 2, CONV_WIDTH // 2)],
        dimension_numbers=('NWC', 'WIO', 'NWC'), feature_group_count=CONV_CH) + b_dw
    z = jax.nn.silu(layer_norm(z, ln_g, ln_b))
    return z @ w_c2 + b_c2


def even_mixer(h_lat, h_ctx, cos, sin, w_in, w_out, lam_q1, lam_k1, lam_q2, lam_k2, subln_g,
               w_pool, pool_scale, lam_init, ctx_out):
    f32 = jnp.float32
    lam = (jnp.exp(jnp.sum(lam_q1.astype(f32) * lam_k1.astype(f32)))
           - jnp.exp(jnp.sum(lam_q2.astype(f32) * lam_k2.astype(f32))) + lam_init)

    def heads(z, tail):
        return z.reshape(z.shape[:2] + (N_DIFF_HEADS,) + tail)

    z_lat = h_lat @ w_in
    q_lat = apply_axial_rope(heads(z_lat[..., :DIFF_WIDTH], (2, DIFF_HEAD_DIM)), cos, sin)
    k_lat = apply_axial_rope(heads(z_lat[..., DIFF_WIDTH:2 * DIFF_WIDTH], (2, DIFF_HEAD_DIM)), cos, sin)
    v_lat = heads(z_lat[..., 2 * DIFF_WIDTH:3 * DIFF_WIDTH], (DIFF_V_DIM,))
    u_lat = z_lat[..., 3 * DIFF_WIDTH:]

    if ctx_out:
        z_ctx = h_ctx @ w_in
        kv_ctx = z_ctx[..., DIFF_WIDTH:3 * DIFF_WIDTH]
    else:
        kv_ctx = h_ctx @ w_in[:, DIFF_WIDTH:3 * DIFF_WIDTH]
    k_ctx = heads(kv_ctx[..., :DIFF_WIDTH], (2, DIFF_HEAD_DIM))
    v_ctx = heads(kv_ctx[..., DIFF_WIDTH:], (DIFF_V_DIM,))
    k_all = jnp.concatenate([k_ctx, k_lat], axis=1)
    v_all = jnp.concatenate([v_ctx, v_lat], axis=1)

    def merge(o, u):
        o = rms_norm(o, subln_g) * (1.0 - lam_init)
        o = o.reshape(o.shape[:2] + (DIFF_WIDTH,))
        return jnp.concatenate([o, multiscale_pool(u, w_pool, pool_scale)], axis=-1) @ w_out

    y_lat = merge(diff_attention(q_lat, k_all, v_all, lam), u_lat)
    y_ctx = None
    if ctx_out:
        q_ctx = heads(z_ctx[..., :DIFF_WIDTH], (2, DIFF_HEAD_DIM))
        y_ctx = merge(diff_attention(q_ctx, k_ctx, v_ctx, lam), z_ctx[..., 3 * DIFF_WIDTH:])
    return y_lat, y_ctx


def setup_inputs(seed: int = 0) -> dict:
    key = jax.random.key(seed)
    ks = jax.random.split(key, 32)
    D = D_MODEL

    def nrm(k, shape, s):
        return jax.random.normal(k, shape, jnp.float32) * s

    return {
        'x': nrm(ks[0], (BATCH, SEQ, D), 1.0),
        'c': nrm(ks[1], (BATCH, D), 1.0),
        'ctx': nrm(ks[2], (BATCH, CTX_LEN, D), 1.0),
        'c_ctx': nrm(ks[3], (D,), 1.0),
        'w_ada': nrm(ks[4], (DEPTH, D, N_MOD * D), D ** -0.5),
        'b_ada': nrm(ks[5], (DEPTH, N_MOD * D), 0.01),
        'ln_g': 1.0 + nrm(ks[6], (DEPTH, 3, D), 0.02),
        'ln_b': nrm(ks[7], (DEPTH, 3, D), 0.02),
        'w_ffn_in': nrm(ks[8], (DEPTH, 2, D, 2 * D_FF), D ** -0.5),
        'w_ffn_out': nrm(ks[9], (DEPTH, 2, D_FF, D), BETA * D_FF ** -0.5),
        'w_mix_in': nrm(ks[10], (N_EVEN, D, MIX_IN_WIDTH), D ** -0.5),
        'w_mix_out': nrm(ks[11], (N_EVEN, MIX_WIDTH, D), BETA * MIX_WIDTH ** -0.5),
        'lam_q1': nrm(ks[12], (N_EVEN, DIFF_HEAD_DIM), 0.1),
        'lam_k1': nrm(ks[13], (N_EVEN, DIFF_HEAD_DIM), 0.1),
        'lam_q2': nrm(ks[14], (N_EVEN, DIFF_HEAD_DIM), 0.1),
        'lam_k2': nrm(ks[15], (N_EVEN, DIFF_HEAD_DIM), 0.1),
        'subln_g': 1.0 + nrm(ks[16], (N_EVEN, DIFF_V_DIM), 0.02),
        'w_pool': nrm(ks[17], (N_EVEN, N_POOL_GROUPS, POOL_GROUP_DIM, POOL_GROUP_DIM), POOL_GROUP_DIM ** -0.5),
        'pool_scale': 1.0 + nrm(ks[18], (N_EVEN, POOL_WIDTH), 0.1),
        'w_c1': nrm(ks[19], (N_ODD, D, 2 * CONV_CH), D ** -0.5),
        'b_c1': nrm(ks[20], (N_ODD, 2 * CONV_CH), 0.01),
        'w_dw': nrm(ks[21], (N_ODD, CONV_WIDTH, CONV_CH), CONV_WIDTH ** -0.5),
        'b_dw': nrm(ks[22], (N_ODD, CONV_CH), 0.01),
        'conv_ln_g': 1.0 + nrm(ks[23], (N_ODD, CONV_CH), 0.02),
        'conv_ln_b': nrm(ks[24], (N_ODD, CONV_CH), 0.02),
        'w_c2': nrm(ks[25], (N_ODD, CONV_CH, D), BETA * CONV_CH ** -0.5),
        'b_c2': nrm(ks[26], (N_ODD, D), 0.01),
    }


def reference(x, c, ctx, c_ctx, w_ada, b_ada, ln_g, ln_b, w_ffn_in, w_ffn_out, w_mix_in, w_mix_out,
              lam_q1, lam_k1, lam_q2, lam_k2, subln_g, w_pool, pool_scale,
              w_c1, b_c1, w_dw, b_dw, conv_ln_g, conv_ln_b, w_c2, b_c2):
    rows = x.shape[1] // GRID_W
    cos, sin = axial_rope_tables(rows)
    silu_c = jax.nn.silu(c)
    silu_cc = jax.nn.silu(c_ctx)
    h_lat, h_ctx = x, ctx
    for l in range(DEPTH):
        even = l % 2 == 0
        ctx_later = any(j % 2 == 0 for j in range(l + 1, DEPTH))
        ctx_here = even or ctx_later
        m_lat = [m[:, None, :] for m in jnp.split(silu_c @ w_ada[l] + b_ada[l], N_MOD, axis=-1)]
        m_ctx = [m[None, None, :] for m in jnp.split(silu_cc @ w_ada[l] + b_ada[l], N_MOD, axis=-1)]

        def sublayer(h, m, i, fn, res_w):
            y = fn(modulate(h, m[3 * i], m[3 * i + 1]))
            return layer_norm(ALPHA * h + res_w * m[3 * i + 2] * y, ln_g[l, i], ln_b[l, i])

        ffn1 = lambda h: swiglu(h, w_ffn_in[l, 0], w_ffn_out[l, 0])
        ffn2 = lambda h: swiglu(h, w_ffn_in[l, 1], w_ffn_out[l, 1])

        h_lat = sublayer(h_lat, m_lat, 0, ffn1, FFN_HALF)
        if ctx_here:
            h_ctx = sublayer(h_ctx, m_ctx, 0, ffn1, FFN_HALF)

        x_lat = modulate(h_lat, m_lat[3], m_lat[4])
        if even:
            e = l // 2
            x_ctx = modulate(h_ctx, m_ctx[3], m_ctx[4])
            y_lat, y_ctx = even_mixer(
                x_lat, x_ctx, cos, sin, w_mix_in[e], w_mix_out[e], lam_q1[e], lam_k1[e], lam_q2[e], lam_k2[e],
                subln_g[e], w_pool[e], pool_scale[e], 0.8 - 0.6 * math.exp(-0.3 * l), ctx_later)
        else:
            o = l // 2
            conv = lambda h: conv_module(h, w_c1[o], b_c1[o], w_dw[o], b_dw[o], conv_ln_g[o], conv_ln_b[o],
                                         w_c2[o], b_c2[o])
            y_lat = conv(x_lat)
            y_ctx = conv(modulate(h_ctx, m_ctx[3], m_ctx[4])) if ctx_later else None
        h_lat = layer_norm(ALPHA * h_lat + m_lat[5] * y_lat, ln_g[l, 1], ln_b[l, 1])
        if ctx_later:
            h_ctx = layer_norm(ALPHA * h_ctx + m_ctx[5] * y_ctx, ln_g[l, 1], ln_b[l, 1])

        h_lat = sublayer(h_lat, m_lat, 2, ffn2, FFN_HALF)
        if ctx_later:
            h_ctx = sublayer(h_ctx, m_ctx, 2, ffn2, FFN_HALF)
    return h_lat
```

```python
import functools
import math

import jax
import jax.numpy as jnp
from jax import lax
from jax.experimental import pallas as pl
from jax.experimental.pallas import tpu as pltpu

D_MODEL = 1024
GRID_W = 64
N_DIFF_HEADS = 4
DIFF_HEAD_DIM = 64
DIFF_V_DIM = 2 * DIFF_HEAD_DIM
DIFF_WIDTH = N_DIFF_HEADS * DIFF_V_DIM
ROPE_FREQS = DIFF_HEAD_DIM // 4
ROPE_THETA = 10000.0
POOL_WINDOWS = (2, 4, 8, 16)
POOL_GROUP_DIM = D_MODEL // 8
POOL_WIDTH = len(POOL_WINDOWS) * POOL_GROUP_DIM
CONV_WIDTH = 31
D_FF = 2816
N_MOD = 9
FFN_HALF = 0.5
LN_EPS = 1e-5

F32 = jnp.float32
BF16 = jnp.bfloat16

SUBLANES = 8
LANES = 128
VMEM_LIMIT_BYTES = 56 * 1024 * 1024

ROW_TILE = 512
ATTN_Q_TILE = 256
ADA_COL_TILE = 1152
POOL_HALO = 8
CONV_HALO = 16
CONV_ROW_CHUNK = 64


def _params(*sem):
    return pltpu.CompilerParams(dimension_semantics=sem, vmem_limit_bytes=VMEM_LIMIT_BYTES)


def _const_spec(shape):
    nd = len(shape)
    return pl.BlockSpec(shape, lambda *_: (0,) * nd, pipeline_mode=pl.Buffered(1))


def _layer_norm(r, g, b):
    mu = jnp.mean(r, axis=-1, keepdims=True)
    d = r - mu
    var = jnp.mean(d * d, axis=-1, keepdims=True)
    return d * lax.rsqrt(var + LN_EPS) * g + b


def _dot(a, b):
    return jnp.dot(a, b, preferred_element_type=F32)


def _dot_nt(a, b):
    return lax.dot_general(a, b, (((1,), (1,)), ((), ())), preferred_element_type=F32)


def _ada_kernel(c_ref, w_ref, b_ref, o_ref):
    s = jax.nn.silu(c_ref[...]).astype(BF16)
    o_ref[0] = _dot(s, w_ref[0].astype(BF16)) + b_ref[0]


def _ada(cc, w_ada, b_ada):
    depth, d, n = w_ada.shape
    r = cc.shape[0]
    return pl.pallas_call(
        _ada_kernel,
        grid=(depth, n // ADA_COL_TILE),
        in_specs=[
            pl.BlockSpec((r, d), lambda l, j: (0, 0)),
            pl.BlockSpec((1, d, ADA_COL_TILE), lambda l, j: (l, 0, j)),
            pl.BlockSpec((1, 1, ADA_COL_TILE), lambda l, j: (l, 0, j)),
        ],
        out_specs=pl.BlockSpec((1, r, ADA_COL_TILE), lambda l, j: (l, 0, j)),
        out_shape=jax.ShapeDtypeStruct((depth, r, n), F32),
        compiler_params=_params("arbitrary", "arbitrary"),
        name="ada",
    )(cc, w_ada, b_ada.reshape(depth, 1, n))


def _mod_spec(mod, tiles_per_batch):
    if mod.shape[0] == 1:
        return pl.BlockSpec((1, N_MOD, D_MODEL), lambda i: (0, 0, 0))
    return pl.BlockSpec((1, N_MOD, D_MODEL), lambda i: (i // tiles_per_batch, 0, 0))


def _modulate(h, mod_ref, i0):
    return h * (1.0 + mod_ref[0, i0 + 1:i0 + 2, :]) + mod_ref[0, i0:i0 + 1, :]


def _ffn_kernel(h_ref, mod_ref, lng_ref, lnb_ref, win_ref, wout_ref, o_ref, *, i0, alpha):
    h = h_ref[...]
    x = _modulate(h, mod_ref, i0).astype(BF16)
    g = _dot(x, win_ref[:, :D_FF])
    u = _dot(x, win_ref[:, D_FF:])
    a = (jax.nn.silu(g) * u).astype(BF16)
    y = _dot(a, wout_ref[...])
    r = alpha * h + (FFN_HALF * mod_ref[0, i0 + 2:i0 + 3, :]) * y
    o_ref[...] = _layer_norm(r, lng_ref[...], lnb_ref[...])


def _ffn(h, mod, ln_g, ln_b, w_in, w_out, *, sub, rows_per_batch, alpha):
    n, d = h.shape
    tm = min(ROW_TILE, rows_per_batch)
    return pl.pallas_call(
        functools.partial(_ffn_kernel, i0=3 * sub, alpha=alpha),
        grid=(n // tm,),
        in_specs=[
            pl.BlockSpec((tm, d), lambda i: (i, 0)),
            _mod_spec(mod, rows_per_batch // tm),
            _const_spec((1, d)), _const_spec((1, d)),
            _const_spec(w_in.shape), _const_spec(w_out.shape),
        ],
        out_specs=pl.BlockSpec((tm, d), lambda i: (i, 0)),
        out_shape=jax.ShapeDtypeStruct((n, d), F32),
        compiler_params=_params("parallel"),
        name="ffn",
    )(h, mod, ln_g.reshape(1, d), ln_b.reshape(1, d), w_in, w_out)


def _rope_tables(seq):
    t = jnp.arange(seq, dtype=F32)
    row = jnp.floor(t / GRID_W)
    col = t - row * GRID_W
    inv_freq = ROPE_THETA ** (-jnp.arange(ROPE_FREQS, dtype=F32) / ROPE_FREQS)
    lane = jnp.arange(LANES)
    seg = lane % DIFF_HEAD_DIM
    pos = jnp.where((seg // (2 * ROPE_FREQS))[None, :] == 0, row[:, None], col[:, None])
    ang = pos * inv_freq[seg % ROPE_FREQS][None, :]
    sign = jnp.where((seg % (2 * ROPE_FREQS)) < ROPE_FREQS, -1.0, 1.0).astype(F32)
    return jnp.cos(ang), jnp.sin(ang) * sign[None, :]


def _mix_in_kernel(h_ref, mod_ref, w_ref, cos_ref, sin_ref, q_ref, k_ref, v_ref, u_ref):
    x = _modulate(h_ref[...], mod_ref, 3).astype(BF16)
    z = _dot(x, w_ref[...])
    cos = cos_ref[...]
    sin = sin_ref[...]
    first_half = (lax.broadcasted_iota(jnp.int32, cos.shape, 1) % (2 * ROPE_FREQS)) < ROPE_FREQS

    def rope(blk):
        swapped = jnp.where(first_half, pltpu.roll(blk, LANES - ROPE_FREQS, 1), pltpu.roll(blk, ROPE_FREQS, 1))
        return blk * cos + swapped * sin

    q_scale = DIFF_HEAD_DIM ** -0.5
    for j in range(DIFF_WIDTH // LANES):
        sl = slice(j * LANES, (j + 1) * LANES)
        q_ref[:, sl] = (rope(z[:, sl]) * q_scale).astype(BF16)
        k_ref[:, sl] = rope(z[:, DIFF_WIDTH + j * LANES:DIFF_WIDTH + (j + 1) * LANES]).astype(BF16)
    v_ref[...] = z[:, 2 * DIFF_WIDTH:3 * DIFF_WIDTH].astype(BF16)
    u_ref[...] = z[:, 3 * DIFF_WIDTH:]


def _mix_in(h, mod, w, cos, sin, *, seq):
    n, d = h.shape
    tm = min(ROW_TILE, seq)
    tpb = seq // tm
    row_spec = lambda width: pl.BlockSpec((tm, width), lambda i: (i, 0))
    tab_spec = pl.BlockSpec((tm, LANES), lambda i: (i % tpb, 0))
    return pl.pallas_call(
        _mix_in_kernel,
        grid=(n // tm,),
        in_specs=[row_spec(d), _mod_spec(mod, tpb), _const_spec(w.shape), tab_spec, tab_spec],
        out_specs=[row_spec(DIFF_WIDTH)] * 3 + [row_spec(POOL_WIDTH)],
        out_shape=[jax.ShapeDtypeStruct((n, DIFF_WIDTH), BF16)] * 3 + [jax.ShapeDtypeStruct((n, POOL_WIDTH), F32)],
        compiler_params=_params("parallel"),
        name="mix_in",
    )(h, mod, w, cos, sin)


def _ctx_kv_kernel(h_ref, mod_ref, w_ref, k_ref, v_ref):
    x = _modulate(h_ref[...], mod_ref, 3).astype(BF16)
    z = _dot(x, w_ref[:, DIFF_WIDTH:3 * DIFF_WIDTH])
    k_ref[...] = z[:, :DIFF_WIDTH].astype(BF16)
    v_ref[...] = z[:, DIFF_WIDTH:].astype(BF16)


def _ctx_kv(h, mod, w, *, seq):
    n, d = h.shape
    tm = min(ROW_TILE, seq)
    row_spec = lambda width: pl.BlockSpec((tm, width), lambda i: (i, 0))
    return pl.pallas_call(
        _ctx_kv_kernel,
        grid=(n // tm,),
        in_specs=[row_spec(d), _mod_spec(mod, seq // tm), _const_spec(w.shape)],
        out_specs=[row_spec(DIFF_WIDTH)] * 2,
        out_shape=[jax.ShapeDtypeStruct((n, DIFF_WIDTH), BF16)] * 2,
        compiler_params=_params("parallel"),
        name="ctx_kv",
    )(h, mod, w)


def _attn_kernel(lq1_ref, lk1_ref, lq2_ref, lk2_ref, g_ref, q_ref, kl_ref, vl_ref, kc_ref, vc_ref, o_ref,
                 *, lam_init):
    lam = (jnp.exp(jnp.sum(lq1_ref[...] * lk1_ref[...], axis=-1, keepdims=True))
           - jnp.exp(jnp.sum(lq2_ref[...] * lk2_ref[...], axis=-1, keepdims=True)) + lam_init)
    lane = lax.broadcasted_iota(jnp.int32, (1, DIFF_V_DIM), 1)
    out_gain = g_ref[...] * (1.0 - lam_init)

    def probs(q, kl, kc):
        sl = _dot_nt(q, kl)
        sc = _dot_nt(q, kc)
        m = jnp.maximum(jnp.max(sl, axis=-1, keepdims=True), jnp.max(sc, axis=-1, keepdims=True))
        el = jnp.exp(sl - m)
        ec = jnp.exp(sc - m)
        return el, ec, jnp.sum(el, axis=-1, keepdims=True) + jnp.sum(ec, axis=-1, keepdims=True)

    for hd in range(N_DIFF_HEADS):
        sl = slice(hd * DIFF_V_DIM, (hd + 1) * DIFF_V_DIM)
        q = q_ref[0, :, sl]
        kl, kc = kl_ref[0, :, sl], kc_ref[0, :, sl]
        e1l, e1c, l1 = probs(jnp.where(lane < DIFF_HEAD_DIM, q, 0), kl, kc)
        e2l, e2c, l2 = probs(jnp.where(lane >= DIFF_HEAD_DIM, q, 0), kl, kc)
        r1 = 1.0 / l1
        r2 = lam / l2
        o = (_dot((e1l * r1 - e2l * r2).astype(BF16), vl_ref[0, :, sl])
             + _dot((e1c * r1 - e2c * r2).astype(BF16), vc_ref[0, :, sl]))
        o = o * lax.rsqrt(jnp.mean(o * o, axis=-1, keepdims=True) + LN_EPS) * out_gain
        o_ref[0, :, sl] = o.astype(BF16)


def _attention(q, k_lat, v_lat, k_ctx, v_ctx, lq1, lk1, lq2, lk2, subln_g, *, lam_init):
    b, t, w = q.shape
    s_ctx = k_ctx.shape[1]
    tq = min(ATTN_Q_TILE, t)
    vec = lambda a: a.reshape(1, -1).astype(F32)
    lam_spec = pl.BlockSpec((1, DIFF_HEAD_DIM), lambda bi, qi: (0, 0))
    return pl.pallas_call(
        functools.partial(_attn_kernel, lam_init=lam_init),
        grid=(b, t // tq),
        in_specs=[
            lam_spec, lam_spec, lam_spec, lam_spec,
            pl.BlockSpec((1, DIFF_V_DIM), lambda bi, qi: (0, 0)),
            pl.BlockSpec((1, tq, w), lambda bi, qi: (bi, qi, 0)),
            pl.BlockSpec((1, t, w), lambda bi, qi: (bi, 0, 0)),
            pl.BlockSpec((1, t, w), lambda bi, qi: (bi, 0, 0)),
            pl.BlockSpec((1, s_ctx, w), lambda bi, qi: (bi, 0, 0)),
            pl.BlockSpec((1, s_ctx, w), lambda bi, qi: (bi, 0, 0)),
        ],
        out_specs=pl.BlockSpec((1, tq, w), lambda bi, qi: (bi, qi, 0)),
        out_shape=jax.ShapeDtypeStruct((b, t, w), BF16),
        compiler_params=_params("parallel", "arbitrary"),
        name="diff_attn",
    )(vec(lq1), vec(lk1), vec(lq2), vec(lk2), vec(subln_g), q, k_lat, v_lat, k_ctx, v_ctx)


def _mix_out_kernel(h_ref, mod_ref, lng_ref, lnb_ref, o_ref, u_ref, up_ref, un_ref, wpool_ref, pscale_ref, wout_ref,
                    out_ref, ext_ref, *, seq, alpha):
    tm = h_ref.shape[0]
    tpb = seq // tm
    ib = pl.program_id(0) % tpb
    ext_ref[0:POOL_HALO, :] = jnp.where(ib > 0, up_ref[...], 0.0)
    ext_ref[POOL_HALO:POOL_HALO + tm, :] = u_ref[...]
    ext_ref[POOL_HALO + tm:, :] = jnp.where(ib < tpb - 1, un_ref[...], 0.0)
    pos = ib * tm + lax.broadcasted_iota(jnp.int32, (tm, 1), 0)

    y = _dot(o_ref[...], wout_ref[:DIFF_WIDTH, :])
    for g, win in enumerate(POOL_WINDOWS):
        sl = slice(g * POOL_GROUP_DIM, (g + 1) * POOL_GROUP_DIM)
        acc = ext_ref[POOL_HALO - win // 2:POOL_HALO - win // 2 + tm, sl]
        for d in range(1 - win // 2, win // 2):
            acc = acc + ext_ref[POOL_HALO + d:POOL_HALO + d + tm, sl]
        cnt = jnp.minimum(pos + win // 2, seq) - jnp.maximum(pos - win // 2, 0)
        pooled = acc / cnt.astype(F32) - u_ref[:, sl]
        yg = _dot(pooled.astype(BF16), wpool_ref[g]) * pscale_ref[:, sl]
        y = y + _dot(yg.astype(BF16), wout_ref[DIFF_WIDTH + g * POOL_GROUP_DIM:DIFF_WIDTH + (g + 1) * POOL_GROUP_DIM, :])
    r = alpha * h_ref[...] + mod_ref[0, 5:6, :] * y
    out_ref[...] = _layer_norm(r, lng_ref[...], lnb_ref[...])


def _mix_out(h, mod, ln_g, ln_b, o, u, w_pool, pool_scale, w_out, *, seq, alpha):
    n, d = h.shape
    tm = min(ROW_TILE, seq)
    hb = tm // POOL_HALO
    last = n // POOL_HALO - 1
    row_spec = lambda width: pl.BlockSpec((tm, width), lambda i: (i, 0))
    return pl.pallas_call(
        functools.partial(_mix_out_kernel, seq=seq, alpha=alpha),
        grid=(n // tm,),
        in_specs=[
            row_spec(d), _mod_spec(mod, seq // tm), _const_spec((1, d)), _const_spec((1, d)),
            row_spec(DIFF_WIDTH), row_spec(POOL_WIDTH),
            pl.BlockSpec((POOL_HALO, POOL_WIDTH), lambda i: (jnp.maximum(i * hb - 1, 0), 0)),
            pl.BlockSpec((POOL_HALO, POOL_WIDTH), lambda i: (jnp.minimum((i + 1) * hb, last), 0)),
            _const_spec(w_pool.shape), _const_spec((1, POOL_WIDTH)), _const_spec(w_out.shape),
        ],
        out_specs=row_spec(d),
        out_shape=jax.ShapeDtypeStruct((n, d), F32),
        scratch_shapes=[pltpu.VMEM((tm + 2 * POOL_HALO, POOL_WIDTH), F32)],
        compiler_params=_params("parallel"),
        name="mix_out",
    )(h, mod, ln_g.reshape(1, d), ln_b.reshape(1, d), o, u, u, u, w_pool, pool_scale.reshape(1, -1), w_out)


def _conv_kernel(h_ref, hp_ref, hn_ref, mod_ref, lng_ref, lnb_ref, wc1_ref, bc1_ref, wdw_ref, bdw_ref,
                 cg_ref, cb_ref, wc2_ref, bc2_ref, out_ref, z_ref, acc_ref, *, seq, alpha):
    tm, d = h_ref.shape
    tpb = seq // tm
    ib = pl.program_id(0) % tpb

    def glu(rows):
        x = _modulate(rows, mod_ref, 3).astype(BF16)
        ag = _dot(x, wc1_ref[...]) + bc1_ref[...]
        return ag[:, :d] * jax.nn.sigmoid(ag[:, d:])

    z_ref[0:CONV_HALO, :] = jnp.where(ib > 0, glu(hp_ref[...]), 0.0)
    z_ref[CONV_HALO:CONV_HALO + tm, :] = glu(h_ref[...])
    z_ref[CONV_HALO + tm:, :] = jnp.where(ib < tpb - 1, glu(hn_ref[...]), 0.0)

    first = CONV_HALO - CONV_WIDTH // 2

    def col_block(c, carry):
        cols = pl.ds(pl.multiple_of(c * LANES, LANES), LANES)
        for r0 in range(0, tm, CONV_ROW_CHUNK):
            acc = jnp.broadcast_to(bdw_ref[:, cols], (CONV_ROW_CHUNK, LANES))
            for k in range(CONV_WIDTH):
                acc = acc + z_ref[pl.ds(r0 + first + k, CONV_ROW_CHUNK), cols] * wdw_ref[k:k + 1, cols]
            acc_ref[pl.ds(r0, CONV_ROW_CHUNK), cols] = acc
        return carry

    lax.fori_loop(0, d // LANES, col_block, 0)

    zc = jax.nn.silu(_layer_norm(acc_ref[...], cg_ref[...], cb_ref[...])).astype(BF16)
    y = _dot(zc, wc2_ref[...]) + bc2_ref[...]
    r = alpha * h_ref[...] + mod_ref[0, 5:6, :] * y
    out_ref[...] = _layer_norm(r, lng_ref[...], lnb_ref[...])


def _conv(h, mod, ln_g, ln_b, w_c1, b_c1, w_dw, b_dw, cg, cb, w_c2, b_c2, *, seq, alpha):
    n, d = h.shape
    tm = min(ROW_TILE, seq)
    hb = tm // CONV_HALO
    last = n // CONV_HALO - 1
    row = lambda a: a.reshape(1, -1)
    return pl.pallas_call(
        functools.partial(_conv_kernel, seq=seq, alpha=alpha),
        grid=(n // tm,),
        in_specs=[
            pl.BlockSpec((tm, d), lambda i: (i, 0)),
            pl.BlockSpec((CONV_HALO, d), lambda i: (jnp.maximum(i * hb - 1, 0), 0)),
            pl.BlockSpec((CONV_HALO, d), lambda i: (jnp.minimum((i + 1) * hb, last), 0)),
            _mod_spec(mod, seq // tm), _const_spec((1, d)), _const_spec((1, d)),
            _const_spec(w_c1.shape), _const_spec((1, 2 * d)), _const_spec(w_dw.shape), _const_spec((1, d)),
            _const_spec((1, d)), _const_spec((1, d)), _const_spec(w_c2.shape), _const_spec((1, d)),
        ],
        out_specs=pl.BlockSpec((tm, d), lambda i: (i, 0)),
        out_shape=jax.ShapeDtypeStruct((n, d), F32),
        scratch_shapes=[pltpu.VMEM((tm + 2 * CONV_HALO, d), F32), pltpu.VMEM((tm, d), F32)],
        compiler_params=_params("parallel"),
        name="conv",
    )(h, h, h, mod, row(ln_g), row(ln_b), w_c1, row(b_c1), w_dw, row(b_dw), row(cg), row(cb), w_c2, row(b_c2))


def kernel(x, c, ctx, c_ctx, w_ada, b_ada, ln_g, ln_b, w_ffn_in, w_ffn_out, w_mix_in, w_mix_out,
           lam_q1, lam_k1, lam_q2, lam_k2, subln_g, w_pool, pool_scale,
           w_c1, b_c1, w_dw, b_dw, conv_ln_g, conv_ln_b, w_c2, b_c2):
    batch, seq, d = x.shape
    ctx_len = ctx.shape[1]
    depth = w_ada.shape[0]
    alpha = (2.0 * depth) ** 0.25
    assert d == D_MODEL and seq % GRID_W == 0
    assert seq % min(ROW_TILE, seq) == 0 and ctx_len % min(ROW_TILE, ctx_len) == 0 and seq % min(ATTN_Q_TILE, seq) == 0

    n_rows = -(-(batch + 1) // SUBLANES) * SUBLANES
    cc = jnp.zeros((n_rows, d), F32).at[:batch].set(c).at[batch].set(c_ctx)
    mods = _ada(cc, w_ada, b_ada)
    cos, sin = _rope_tables(seq)

    bf = lambda w: w.astype(BF16)
    w_ffn_in, w_ffn_out, w_mix_in, w_mix_out, w_pool = bf(w_ffn_in), bf(w_ffn_out), bf(w_mix_in), bf(w_mix_out), bf(w_pool)
    w_c1, w_c2 = bf(w_c1), bf(w_c2)

    h_lat = x.reshape(batch * seq, d)
    h_ctx = ctx.reshape(batch * ctx_len, d)
    for l in range(depth):
        even = l % 2 == 0
        ctx_later = any(j % 2 == 0 for j in range(l + 1, depth))
        m_lat = mods[l, :batch].reshape(batch, N_MOD, d)
        m_ctx = mods[l, batch].reshape(1, N_MOD, d)
        ffn = lambda h, m, sub, j, rpb: _ffn(h, m, ln_g[l, sub], ln_b[l, sub], w_ffn_in[l, j], w_ffn_out[l, j],
                                             sub=sub, rows_per_batch=rpb, alpha=alpha)

        h_lat = ffn(h_lat, m_lat, 0, 0, seq)
        if even or ctx_later:
            h_ctx = ffn(h_ctx, m_ctx, 0, 0, ctx_len)

        if even:
            e = l // 2
            lam_init = 0.8 - 0.6 * math.exp(-0.3 * l)
            q, k_lat, v_lat, u = _mix_in(h_lat, m_lat, w_mix_in[e], cos, sin, seq=seq)
            if ctx_later:
                raise NotImplementedError("context outputs of an even mixer are not needed at this depth")
            k_ctx, v_ctx = _ctx_kv(h_ctx, m_ctx, w_mix_in[e], seq=ctx_len)
            lat3 = lambda a: a.reshape(batch, seq, DIFF_WIDTH)
            ctx3 = lambda a: a.reshape(batch, ctx_len, DIFF_WIDTH)
            o = _attention(lat3(q), lat3(k_lat), lat3(v_lat), ctx3(k_ctx), ctx3(v_ctx),
                           lam_q1[e], lam_k1[e], lam_q2[e], lam_k2[e], subln_g[e], lam_init=lam_init)
            h_lat = _mix_out(h_lat, m_lat, ln_g[l, 1], ln_b[l, 1], o.reshape(batch * seq, DIFF_WIDTH), u,
                             w_pool[e], pool_scale[e], w_mix_out[e], seq=seq, alpha=alpha)
        else:
            o = l // 2
            if ctx_later:
                raise NotImplementedError("context outputs of an odd mixer are not needed at this depth")
            h_lat = _conv(h_lat, m_lat, ln_g[l, 1], ln_b[l, 1], w_c1[o], b_c1[o], w_dw[o], b_dw[o],
                          conv_ln_g[o], conv_ln_b[o], w_c2[o], b_c2[o], seq=seq, alpha=alpha)

        h_lat = ffn(h_lat, m_lat, 2, 1, seq)
        if ctx_later:
            h_ctx = ffn(h_ctx, m_ctx, 2, 1, ctx_len)
    return h_lat.reshape(batch, seq, d)
```

```python
import functools
import math

import jax
import jax.numpy as jnp
from jax import lax
from jax.experimental import pallas as pl
from jax.experimental.pallas import tpu as pltpu

D_MODEL = 1024
GRID_W = 64
N_DIFF_HEADS = 4
DIFF_HEAD_DIM = 64
DIFF_V_DIM = 2 * DIFF_HEAD_DIM
DIFF_WIDTH = N_DIFF_HEADS * DIFF_V_DIM
ROPE_FREQS = DIFF_HEAD_DIM // 4
ROPE_THETA = 10000.0
POOL_WINDOWS = (2, 4, 8, 16)
POOL_GROUP_DIM = D_MODEL // 8
POOL_WIDTH = len(POOL_WINDOWS) * POOL_GROUP_DIM
CONV_WIDTH = 31
D_FF = 2816
N_MOD = 9
FFN_HALF = 0.5
LN_EPS = 1e-5

F32 = jnp.float32
BF16 = jnp.bfloat16

SUBLANES = 8
LANES = 128
VMEM_LIMIT_BYTES = 56 * 1024 * 1024

ROW_TILE = 512
ATTN_Q_TILE = 256
ATTN_KEY_CHUNK = 512
ADA_COL_TILE = 1152
POOL_HALO = 8
CONV_HALO = 16
CONV_ROW_CHUNK = 64


def _params(*sem):
    return pltpu.CompilerParams(dimension_semantics=sem, vmem_limit_bytes=VMEM_LIMIT_BYTES)


def _const_spec(shape):
    nd = len(shape)
    return pl.BlockSpec(shape, lambda *_: (0,) * nd, pipeline_mode=pl.Buffered(1))


def _layer_norm(r, g, b):
    mu = jnp.mean(r, axis=-1, keepdims=True)
    d = r - mu
    var = jnp.mean(d * d, axis=-1, keepdims=True)
    return d * lax.rsqrt(var + LN_EPS) * g + b


def _dot(a, b):
    return jnp.dot(a, b, preferred_element_type=F32)


def _dot_nt(a, b):
    return lax.dot_general(a, b, (((1,), (1,)), ((), ())), preferred_element_type=F32)


def _ada_kernel(c_ref, w_ref, b_ref, o_ref):
    s = jax.nn.silu(c_ref[...]).astype(BF16)
    o_ref[0] = _dot(s, w_ref[0].astype(BF16)) + b_ref[0]


def _ada(cc, w_ada, b_ada):
    depth, d, n = w_ada.shape
    r = cc.shape[0]
    return pl.pallas_call(
        _ada_kernel,
        grid=(depth, n // ADA_COL_TILE),
        in_specs=[
            pl.BlockSpec((r, d), lambda l, j: (0, 0)),
            pl.BlockSpec((1, d, ADA_COL_TILE), lambda l, j: (l, 0, j)),
            pl.BlockSpec((1, 1, ADA_COL_TILE), lambda l, j: (l, 0, j)),
        ],
        out_specs=pl.BlockSpec((1, r, ADA_COL_TILE), lambda l, j: (l, 0, j)),
        out_shape=jax.ShapeDtypeStruct((depth, r, n), F32),
        compiler_params=_params("arbitrary", "arbitrary"),
        name="ada",
    )(cc, w_ada, b_ada.reshape(depth, 1, n))


def _mod_spec(mod, tiles_per_batch):
    if mod.shape[0] == 1:
        return pl.BlockSpec((1, N_MOD, D_MODEL), lambda i: (0, 0, 0))
    return pl.BlockSpec((1, N_MOD, D_MODEL), lambda i: (i // tiles_per_batch, 0, 0))


def _modulate(h, mod_ref, i0):
    return h * (1.0 + mod_ref[0, i0 + 1:i0 + 2, :]) + mod_ref[0, i0:i0 + 1, :]


def _ffn_kernel(h_ref, mod_ref, lng_ref, lnb_ref, win_ref, wout_ref, o_ref, *, i0, alpha):
    h = h_ref[...]
    x = _modulate(h, mod_ref, i0).astype(BF16)
    g = _dot(x, win_ref[:, :D_FF])
    u = _dot(x, win_ref[:, D_FF:])
    a = (jax.nn.silu(g) * u).astype(BF16)
    y = _dot(a, wout_ref[...])
    r = alpha * h + (FFN_HALF * mod_ref[0, i0 + 2:i0 + 3, :]) * y
    o_ref[...] = _layer_norm(r, lng_ref[...], lnb_ref[...])


def _ffn(h, mod, ln_g, ln_b, w_in, w_out, *, sub, rows_per_batch, alpha):
    n, d = h.shape
    tm = min(ROW_TILE, rows_per_batch)
    return pl.pallas_call(
        functools.partial(_ffn_kernel, i0=3 * sub, alpha=alpha),
        grid=(n // tm,),
        in_specs=[
            pl.BlockSpec((tm, d), lambda i: (i, 0)),
            _mod_spec(mod, rows_per_batch // tm),
            _const_spec((1, d)), _const_spec((1, d)),
            _const_spec(w_in.shape), _const_spec(w_out.shape),
        ],
        out_specs=pl.BlockSpec((tm, d), lambda i: (i, 0)),
        out_shape=jax.ShapeDtypeStruct((n, d), F32),
        compiler_params=_params("parallel"),
        name="ffn",
    )(h, mod, ln_g.reshape(1, d), ln_b.reshape(1, d), w_in, w_out)


def _rope_tables(seq):
    t = jnp.arange(seq, dtype=F32)
    row = jnp.floor(t / GRID_W)
    col = t - row * GRID_W
    inv_freq = ROPE_THETA ** (-jnp.arange(ROPE_FREQS, dtype=F32) / ROPE_FREQS)
    lane = jnp.arange(LANES)
    seg = lane % DIFF_HEAD_DIM
    pos = jnp.where((seg // (2 * ROPE_FREQS))[None, :] == 0, row[:, None], col[:, None])
    ang = pos * inv_freq[seg % ROPE_FREQS][None, :]
    sign = jnp.where((seg % (2 * ROPE_FREQS)) < ROPE_FREQS, -1.0, 1.0).astype(F32)
    return jnp.cos(ang), jnp.sin(ang) * sign[None, :]


def _mix_in_kernel(h_ref, mod_ref, w_ref, cos_ref, sin_ref, q_ref, k_ref, vt_ref, u_ref):
    x = _modulate(h_ref[...], mod_ref, 3).astype(BF16)
    z = _dot(x, w_ref[...])
    cos = cos_ref[...]
    sin = sin_ref[...]
    first_half = (lax.broadcasted_iota(jnp.int32, cos.shape, 1) % (2 * ROPE_FREQS)) < ROPE_FREQS

    def rope(blk):
        swapped = jnp.where(first_half, pltpu.roll(blk, LANES - ROPE_FREQS, 1), pltpu.roll(blk, ROPE_FREQS, 1))
        return blk * cos + swapped * sin

    q_scale = DIFF_HEAD_DIM ** -0.5 * math.log2(math.e)
    for j in range(DIFF_WIDTH // LANES):
        sl = slice(j * LANES, (j + 1) * LANES)
        q_ref[:, sl] = (rope(z[:, sl]) * q_scale).astype(BF16)
        k_ref[:, sl] = rope(z[:, DIFF_WIDTH + j * LANES:DIFF_WIDTH + (j + 1) * LANES]).astype(BF16)
    vt_ref[0] = z[:, 2 * DIFF_WIDTH:3 * DIFF_WIDTH].T.astype(BF16)
    u_ref[...] = z[:, 3 * DIFF_WIDTH:]


def _mix_in(h, mod, w, cos, sin, *, seq):
    n, d = h.shape
    tm = min(ROW_TILE, seq)
    tpb = seq // tm
    row_spec = lambda width: pl.BlockSpec((tm, width), lambda i: (i, 0))
    tab_spec = pl.BlockSpec((tm, LANES), lambda i: (i % tpb, 0))
    vt_spec = pl.BlockSpec((1, DIFF_WIDTH, tm), lambda i: (i // tpb, 0, i % tpb))
    return pl.pallas_call(
        _mix_in_kernel,
        grid=(n // tm,),
        in_specs=[row_spec(d), _mod_spec(mod, tpb), _const_spec(w.shape), tab_spec, tab_spec],
        out_specs=[row_spec(DIFF_WIDTH), row_spec(DIFF_WIDTH), vt_spec, row_spec(POOL_WIDTH)],
        out_shape=[jax.ShapeDtypeStruct((n, DIFF_WIDTH), BF16), jax.ShapeDtypeStruct((n, DIFF_WIDTH), BF16),
                   jax.ShapeDtypeStruct((n // seq, DIFF_WIDTH, seq), BF16), jax.ShapeDtypeStruct((n, POOL_WIDTH), F32)],
        compiler_params=_params("parallel"),
        name="mix_in",
    )(h, mod, w, cos, sin)


def _ctx_kv_kernel(h_ref, mod_ref, w_ref, k_ref, vt_ref):
    x = _modulate(h_ref[...], mod_ref, 3).astype(BF16)
    z = _dot(x, w_ref[:, DIFF_WIDTH:3 * DIFF_WIDTH])
    k_ref[...] = z[:, :DIFF_WIDTH].astype(BF16)
    vt_ref[0] = z[:, DIFF_WIDTH:].T.astype(BF16)


def _ctx_kv(h, mod, w, *, seq):
    n, d = h.shape
    tm = min(ROW_TILE, seq)
    tpb = seq // tm
    return pl.pallas_call(
        _ctx_kv_kernel,
        grid=(n // tm,),
        in_specs=[pl.BlockSpec((tm, d), lambda i: (i, 0)), _mod_spec(mod, tpb), _const_spec(w.shape)],
        out_specs=[pl.BlockSpec((tm, DIFF_WIDTH), lambda i: (i, 0)),
                   pl.BlockSpec((1, DIFF_WIDTH, tm), lambda i: (i // tpb, 0, i % tpb))],
        out_shape=[jax.ShapeDtypeStruct((n, DIFF_WIDTH), BF16), jax.ShapeDtypeStruct((n // seq, DIFF_WIDTH, seq), BF16)],
        compiler_params=_params("parallel"),
        name="ctx_kv",
    )(h, mod, w)


def _attn_kernel(lq1_ref, lk1_ref, lq2_ref, lk2_ref, g_ref, q_ref, kl_ref, vtl_ref, kc_ref, vtc_ref, o_ref,
                 s_scr, e_scr, *, lam_init):
    t, s_ctx = kl_ref.shape[1], kc_ref.shape[1]
    lam = (jnp.exp(jnp.sum(lq1_ref[...] * lk1_ref[...], axis=-1, keepdims=True))
           - jnp.exp(jnp.sum(lq2_ref[...] * lk2_ref[...], axis=-1, keepdims=True)) + lam_init)
    lane = lax.broadcasted_iota(jnp.int32, (1, DIFF_V_DIM), 1)
    out_gain = g_ref[...] * (1.0 - lam_init)
    units = [(hd, mp) for hd in range(N_DIFF_HEADS) for mp in range(2)]
    chunks = [(kc_ref, vtc_ref, 0, 0, s_ctx)]
    chunks += [(kl_ref, vtl_ref, c * ATTN_KEY_CHUNK, s_ctx + c * ATTN_KEY_CHUNK, ATTN_KEY_CHUNK)
               for c in range(t // ATTN_KEY_CHUNK)]
    col_max, col_sum, acc, norm = {}, {}, {}, {}

    def scores(ui, ci):
        hd, mp = units[ui]
        cols = slice(hd * DIFF_V_DIM, (hd + 1) * DIFF_V_DIM)
        k_ref, _, off, soff, n = chunks[ci]
        q = jnp.where((lane < DIFF_HEAD_DIM) if mp == 0 else (lane >= DIFF_HEAD_DIM), q_ref[0, :, cols], 0)
        s = _dot_nt(k_ref[0, off:off + n, cols], q)
        s_scr[ui % 2, soff:soff + n, :] = s
        m = jnp.max(s, axis=0, keepdims=True)
        col_max[ui] = m if ci == 0 else jnp.maximum(col_max[ui], m)

    def exponentials(ui, ci):
        _, _, _, soff, n = chunks[ci]
        e = jnp.exp2(s_scr[ui % 2, soff:soff + n, :] - col_max[ui])
        e_scr[ui % 4, soff:soff + n, :] = e.astype(BF16)
        l = jnp.sum(e, axis=0, keepdims=True)
        col_sum[ui] = l if ci == 0 else col_sum[ui] + l

    def values(hd, ci):
        rows = slice(hd * DIFF_V_DIM, (hd + 1) * DIFF_V_DIM)
        _, vt_ref, off, soff, n = chunks[ci]
        if ci == 0:
            l1, l2 = col_sum.pop(2 * hd), col_sum.pop(2 * hd + 1)
            norm[hd] = ((lam * l1 / l2).astype(BF16), l1)
        p = e_scr[(2 * hd) % 4, soff:soff + n, :] - e_scr[(2 * hd + 1) % 4, soff:soff + n, :] * norm[hd][0]
        d = _dot(vt_ref[0, rows, off:off + n], p)
        acc[hd] = d if ci == 0 else acc[hd] + d
        if ci == len(chunks) - 1:
            o = acc.pop(hd) / norm.pop(hd)[1]
            o = o * lax.rsqrt(jnp.mean(o * o, axis=0, keepdims=True) + LN_EPS) * out_gain
            o_ref[0, :, rows] = o.T.astype(BF16)

    for r in range(len(units) + 2):
        for ci in range(len(chunks)):
            if r < len(units):
                scores(r, ci)
            if 1 <= r <= len(units):
                exponentials(r - 1, ci)
            if r >= 3 and r % 2 == 1:
                values((r - 3) // 2, ci)


def _attention(q, k_lat, vt_lat, k_ctx, vt_ctx, lq1, lk1, lq2, lk2, subln_g, *, lam_init):
    b, t, w = q.shape
    s_ctx = k_ctx.shape[1]
    tq = min(ATTN_Q_TILE, t)
    assert t % ATTN_KEY_CHUNK == 0
    vec = lambda a: a.reshape(1, -1).astype(F32)
    lam_spec = pl.BlockSpec((1, DIFF_HEAD_DIM), lambda bi, qi: (0, 0))
    return pl.pallas_call(
        functools.partial(_attn_kernel, lam_init=lam_init),
        grid=(b, t // tq),
        in_specs=[
            lam_spec, lam_spec, lam_spec, lam_spec,
            pl.BlockSpec((DIFF_V_DIM, 1), lambda bi, qi: (0, 0)),
            pl.BlockSpec((1, tq, w), lambda bi, qi: (bi, qi, 0)),
            pl.BlockSpec((1, t, w), lambda bi, qi: (bi, 0, 0)),
            pl.BlockSpec((1, w, t), lambda bi, qi: (bi, 0, 0)),
            pl.BlockSpec((1, s_ctx, w), lambda bi, qi: (bi, 0, 0)),
            pl.BlockSpec((1, w, s_ctx), lambda bi, qi: (bi, 0, 0)),
        ],
        out_specs=pl.BlockSpec((1, tq, w), lambda bi, qi: (bi, qi, 0)),
        out_shape=jax.ShapeDtypeStruct((b, t, w), BF16),
        scratch_shapes=[pltpu.VMEM((2, t + s_ctx, tq), F32), pltpu.VMEM((4, t + s_ctx, tq), BF16)],
        compiler_params=_params("parallel", "arbitrary"),
        name="diff_attn",
    )(vec(lq1), vec(lk1), vec(lq2), vec(lk2), subln_g.reshape(DIFF_V_DIM, 1).astype(F32),
      q, k_lat, vt_lat, k_ctx, vt_ctx)


def _mix_out_kernel(h_ref, mod_ref, lng_ref, lnb_ref, o_ref, u_ref, up_ref, un_ref, wpool_ref, pscale_ref, wout_ref,
                    out_ref, ext_ref, *, seq, alpha):
    tm = h_ref.shape[0]
    tpb = seq // tm
    ib = pl.program_id(0) % tpb
    ext_ref[0:POOL_HALO, :] = jnp.where(ib > 0, up_ref[...], 0.0)
    ext_ref[POOL_HALO:POOL_HALO + tm, :] = u_ref[...]
    ext_ref[POOL_HALO + tm:, :] = jnp.where(ib < tpb - 1, un_ref[...], 0.0)
    pos = ib * tm + lax.broadcasted_iota(jnp.int32, (tm, 1), 0)

    y = _dot(o_ref[...], wout_ref[:DIFF_WIDTH, :])
    for g, win in enumerate(POOL_WINDOWS):
        sl = slice(g * POOL_GROUP_DIM, (g + 1) * POOL_GROUP_DIM)
        acc = ext_ref[POOL_HALO - win // 2:POOL_HALO - win // 2 + tm, sl]
        for d in range(1 - win // 2, win // 2):
            acc = acc + ext_ref[POOL_HALO + d:POOL_HALO + d + tm, sl]
        cnt = jnp.minimum(pos + win // 2, seq) - jnp.maximum(pos - win // 2, 0)
        pooled = acc / cnt.astype(F32) - u_ref[:, sl]
        yg = _dot(pooled.astype(BF16), wpool_ref[g]) * pscale_ref[:, sl]
        y = y + _dot(yg.astype(BF16), wout_ref[DIFF_WIDTH + g * POOL_GROUP_DIM:DIFF_WIDTH + (g + 1) * POOL_GROUP_DIM, :])
    r = alpha * h_ref[...] + mod_ref[0, 5:6, :] * y
    out_ref[...] = _layer_norm(r, lng_ref[...], lnb_ref[...])


def _mix_out(h, mod, ln_g, ln_b, o, u, w_pool, pool_scale, w_out, *, seq, alpha):
    n, d = h.shape
    tm = min(ROW_TILE, seq)
    hb = tm // POOL_HALO
    last = n // POOL_HALO - 1
    row_spec = lambda width: pl.BlockSpec((tm, width), lambda i: (i, 0))
    return pl.pallas_call(
        functools.partial(_mix_out_kernel, seq=seq, alpha=alpha),
        grid=(n // tm,),
        in_specs=[
            row_spec(d), _mod_spec(mod, seq // tm), _const_spec((1, d)), _const_spec((1, d)),
            row_spec(DIFF_WIDTH), row_spec(POOL_WIDTH),
            pl.BlockSpec((POOL_HALO, POOL_WIDTH), lambda i: (jnp.maximum(i * hb - 1, 0), 0)),
            pl.BlockSpec((POOL_HALO, POOL_WIDTH), lambda i: (jnp.minimum((i + 1) * hb, last), 0)),
            _const_spec(w_pool.shape), _const_spec((1, POOL_WIDTH)), _const_spec(w_out.shape),
        ],
        out_specs=row_spec(d),
        out_shape=jax.ShapeDtypeStruct((n, d), F32),
        scratch_shapes=[pltpu.VMEM((tm + 2 * POOL_HALO, POOL_WIDTH), F32)],
        compiler_params=_params("parallel"),
        name="mix_out",
    )(h, mod, ln_g.reshape(1, d), ln_b.reshape(1, d), o, u, u, u, w_pool, pool_scale.reshape(1, -1), w_out)


def _conv_kernel(h_ref, hp_ref, hn_ref, mod_ref, lng_ref, lnb_ref, wc1_ref, bc1_ref, wdw_ref, bdw_ref,
                 cg_ref, cb_ref, wc2_ref, bc2_ref, out_ref, z_ref, acc_ref, *, seq, alpha):
    tm, d = h_ref.shape
    tpb = seq // tm
    ib = pl.program_id(0) % tpb

    def glu(rows):
        x = _modulate(rows, mod_ref, 3).astype(BF16)
        ag = _dot(x, wc1_ref[...]) + bc1_ref[...]
        return ag[:, :d] * jax.nn.sigmoid(ag[:, d:])

    z_ref[0:CONV_HALO, :] = jnp.where(ib > 0, glu(hp_ref[...]), 0.0)
    z_ref[CONV_HALO:CONV_HALO + tm, :] = glu(h_ref[...])
    z_ref[CONV_HALO + tm:, :] = jnp.where(ib < tpb - 1, glu(hn_ref[...]), 0.0)

    first = CONV_HALO - CONV_WIDTH // 2
    span = CONV_ROW_CHUNK + SUBLANES

    def col_block(c, carry):
        cols = pl.ds(pl.multiple_of(c * LANES, LANES), LANES)
        for r0 in range(0, tm, CONV_ROW_CHUNK):
            acc = jnp.broadcast_to(bdw_ref[:, cols], (CONV_ROW_CHUNK, LANES))
            for b in range(SUBLANES):
                part = None
                for a in range(-(-(first + CONV_WIDTH) // SUBLANES)):
                    k = SUBLANES * a + b - first
                    if 0 <= k < CONV_WIDTH:
                        term = z_ref[pl.ds(r0 + SUBLANES * a, span), cols] * wdw_ref[k:k + 1, cols]
                        part = term if part is None else part + term
                acc = acc + part[b:b + CONV_ROW_CHUNK]
            acc_ref[pl.ds(r0, CONV_ROW_CHUNK), cols] = acc
        return carry

    lax.fori_loop(0, d // LANES, col_block, 0)

    zc = jax.nn.silu(_layer_norm(acc_ref[...], cg_ref[...], cb_ref[...])).astype(BF16)
    y = _dot(zc, wc2_ref[...]) + bc2_ref[...]
    r = alpha * h_ref[...] + mod_ref[0, 5:6, :] * y
    out_ref[...] = _layer_norm(r, lng_ref[...], lnb_ref[...])


def _conv(h, mod, ln_g, ln_b, w_c1, b_c1, w_dw, b_dw, cg, cb, w_c2, b_c2, *, seq, alpha):
    n, d = h.shape
    tm = min(ROW_TILE, seq)
    hb = tm // CONV_HALO
    last = n // CONV_HALO - 1
    row = lambda a: a.reshape(1, -1)
    return pl.pallas_call(
        functools.partial(_conv_kernel, seq=seq, alpha=alpha),
        grid=(n // tm,),
        in_specs=[
            pl.BlockSpec((tm, d), lambda i: (i, 0)),
            pl.BlockSpec((CONV_HALO, d), lambda i: (jnp.maximum(i * hb - 1, 0), 0)),
            pl.BlockSpec((CONV_HALO, d), lambda i: (jnp.minimum((i + 1) * hb, last), 0)),
            _mod_spec(mod, seq // tm), _const_spec((1, d)), _const_spec((1, d)),
            _const_spec(w_c1.shape), _const_spec((1, 2 * d)), _const_spec(w_dw.shape), _const_spec((1, d)),
            _const_spec((1, d)), _const_spec((1, d)), _const_spec(w_c2.shape), _const_spec((1, d)),
        ],
        out_specs=pl.BlockSpec((tm, d), lambda i: (i, 0)),
        out_shape=jax.ShapeDtypeStruct((n, d), F32),
        scratch_shapes=[pltpu.VMEM((tm + 2 * CONV_HALO, d), F32), pltpu.VMEM((tm, d), F32)],
        compiler_params=_params("parallel"),
        name="conv",
    )(h, h, h, mod, row(ln_g), row(ln_b), w_c1, row(b_c1), w_dw, row(b_dw), row(cg), row(cb), w_c2, row(b_c2))


def kernel(x, c, ctx, c_ctx, w_ada, b_ada, ln_g, ln_b, w_ffn_in, w_ffn_out, w_mix_in, w_mix_out,
           lam_q1, lam_k1, lam_q2, lam_k2, subln_g, w_pool, pool_scale,
           w_c1, b_c1, w_dw, b_dw, conv_ln_g, conv_ln_b, w_c2, b_c2):
    batch, seq, d = x.shape
    ctx_len = ctx.shape[1]
    depth = w_ada.shape[0]
    alpha = (2.0 * depth) ** 0.25
    assert d == D_MODEL and seq % GRID_W == 0
    assert seq % min(ROW_TILE, seq) == 0 and ctx_len % min(ROW_TILE, ctx_len) == 0 and seq % min(ATTN_Q_TILE, seq) == 0

    n_rows = -(-(batch + 1) // SUBLANES) * SUBLANES
    cc = jnp.zeros((n_rows, d), F32).at[:batch].set(c).at[batch].set(c_ctx)
    mods = _ada(cc, w_ada, b_ada)
    cos, sin = _rope_tables(seq)

    bf = lambda w: w.astype(BF16)
    w_ffn_in, w_ffn_out, w_mix_in, w_mix_out, w_pool = bf(w_ffn_in), bf(w_ffn_out), bf(w_mix_in), bf(w_mix_out), bf(w_pool)
    w_c1, w_c2 = bf(w_c1), bf(w_c2)

    h_lat = x.reshape(batch * seq, d)
    h_ctx = ctx.reshape(batch * ctx_len, d)
    for l in range(depth):
        even = l % 2 == 0
        ctx_later = any(j % 2 == 0 for j in range(l + 1, depth))
        m_lat = mods[l, :batch].reshape(batch, N_MOD, d)
        m_ctx = mods[l, batch].reshape(1, N_MOD, d)
        ffn = lambda h, m, sub, j, rpb: _ffn(h, m, ln_g[l, sub], ln_b[l, sub], w_ffn_in[l, j], w_ffn_out[l, j],
                                             sub=sub, rows_per_batch=rpb, alpha=alpha)

        h_lat = ffn(h_lat, m_lat, 0, 0, seq)
        if even or ctx_later:
            h_ctx = ffn(h_ctx, m_ctx, 0, 0, ctx_len)

        if even:
            e = l // 2
            lam_init = 0.8 - 0.6 * math.exp(-0.3 * l)
            q, k_lat, vt_lat, u = _mix_in(h_lat, m_lat, w_mix_in[e], cos, sin, seq=seq)
            if ctx_later:
                raise NotImplementedError("context outputs of an even mixer are not needed at this depth")
            k_ctx, vt_ctx = _ctx_kv(h_ctx, m_ctx, w_mix_in[e], seq=ctx_len)
            lat3 = lambda a: a.reshape(batch, seq, DIFF_WIDTH)
            o = _attention(lat3(q), lat3(k_lat), vt_lat, k_ctx.reshape(batch, ctx_len, DIFF_WIDTH), vt_ctx,
                           lam_q1[e], lam_k1[e], lam_q2[e], lam_k2[e], subln_g[e], lam_init=lam_init)
            h_lat = _mix_out(h_lat, m_lat, ln_g[l, 1], ln_b[l, 1], o.reshape(batch * seq, DIFF_WIDTH), u,
                             w_pool[e], pool_scale[e], w_mix_out[e], seq=seq, alpha=alpha)
        else:
            o = l // 2
            if ctx_later:
                raise NotImplementedError("context outputs of an odd mixer are not needed at this depth")
            h_lat = _conv(h_lat, m_lat, ln_g[l, 1], ln_b[l, 1], w_c1[o], b_c1[o], w_dw[o], b_dw[o],
                          conv_ln_g[o], conv_ln_b[o], w_c2[o], b_c2[o], seq=seq, alpha=alpha)

        h_lat = ffn(h_lat, m_lat, 2, 1, seq)
        if ctx_later:
            h_ctx = ffn(h_ctx, m_ctx, 2, 1, ctx_len)
    return h_lat.reshape(batch, seq, d)
```

```python
import functools
import math

import jax
import jax.numpy as jnp
from jax import lax
from jax.experimental import pallas as pl
from jax.experimental.pallas import tpu as pltpu

D_MODEL = 1024
GRID_W = 64
N_DIFF_HEADS = 4
DIFF_HEAD_DIM = 64
DIFF_V_DIM = 2 * DIFF_HEAD_DIM
DIFF_WIDTH = N_DIFF_HEADS * DIFF_V_DIM
ROPE_FREQS = DIFF_HEAD_DIM // 4
ROPE_THETA = 10000.0
POOL_WINDOWS = (2, 4, 8, 16)
POOL_GROUP_DIM = D_MODEL // 8
POOL_WIDTH = len(POOL_WINDOWS) * POOL_GROUP_DIM
CONV_WIDTH = 31
D_FF = 2816
N_MOD = 9
FFN_HALF = 0.5
LN_EPS = 1e-5

F32 = jnp.float32
BF16 = jnp.bfloat16

SUBLANES = 8
LANES = 128
VMEM_LIMIT_BYTES = 56 * 1024 * 1024

ROW_TILE = 512
ATTN_Q_TILE = 256
ATTN_KEY_CHUNK = 512
ADA_COL_TILE = 1152
POOL_HALO = 8
CONV_HALO = 16
CONV_ROW_CHUNK = 64
CONV_CH_BLOCK = 256


def _params(*sem):
    return pltpu.CompilerParams(dimension_semantics=sem, vmem_limit_bytes=VMEM_LIMIT_BYTES)


def _const_spec(shape):
    nd = len(shape)
    return pl.BlockSpec(shape, lambda *_: (0,) * nd, pipeline_mode=pl.Buffered(1))


def _layer_norm(r, g, b):
    mu = jnp.mean(r, axis=-1, keepdims=True)
    d = r - mu
    var = jnp.mean(d * d, axis=-1, keepdims=True)
    return d * lax.rsqrt(var + LN_EPS) * g + b


def _dot(a, b):
    return jnp.dot(a, b, preferred_element_type=F32)


def _dot_nt(a, b):
    return lax.dot_general(a, b, (((1,), (1,)), ((), ())), preferred_element_type=F32)


def _ada_kernel(c_ref, w_ref, b_ref, o_ref):
    s = jax.nn.silu(c_ref[...]).astype(BF16)
    o_ref[0] = _dot(s, w_ref[0].astype(BF16)) + b_ref[0]


def _ada(cc, w_ada, b_ada):
    depth, d, n = w_ada.shape
    r = cc.shape[0]
    return pl.pallas_call(
        _ada_kernel,
        grid=(depth, n // ADA_COL_TILE),
        in_specs=[
            pl.BlockSpec((r, d), lambda l, j: (0, 0)),
            pl.BlockSpec((1, d, ADA_COL_TILE), lambda l, j: (l, 0, j)),
            pl.BlockSpec((1, 1, ADA_COL_TILE), lambda l, j: (l, 0, j)),
        ],
        out_specs=pl.BlockSpec((1, r, ADA_COL_TILE), lambda l, j: (l, 0, j)),
        out_shape=jax.ShapeDtypeStruct((depth, r, n), F32),
        compiler_params=_params("arbitrary", "arbitrary"),
        name="ada",
    )(cc, w_ada, b_ada.reshape(depth, 1, n))


def _mod_spec(mod, tiles_per_batch):
    if mod.shape[0] == 1:
        return pl.BlockSpec((1, N_MOD, D_MODEL), lambda i: (0, 0, 0))
    return pl.BlockSpec((1, N_MOD, D_MODEL), lambda i: (i // tiles_per_batch, 0, 0))


def _modulate(h, mod_ref, i0):
    return h * (1.0 + mod_ref[0, i0 + 1:i0 + 2, :]) + mod_ref[0, i0:i0 + 1, :]


def _ffn_kernel(h_ref, mod_ref, lng_ref, lnb_ref, win_ref, wout_ref, o_ref, *, i0, alpha):
    h = h_ref[...]
    x = _modulate(h, mod_ref, i0).astype(BF16)
    g = _dot(x, win_ref[:, :D_FF])
    u = _dot(x, win_ref[:, D_FF:])
    a = (jax.nn.silu(g) * u).astype(BF16)
    y = _dot(a, wout_ref[...])
    r = alpha * h + (FFN_HALF * mod_ref[0, i0 + 2:i0 + 3, :]) * y
    o_ref[...] = _layer_norm(r, lng_ref[...], lnb_ref[...])


def _ffn(h, mod, ln_g, ln_b, w_in, w_out, *, layer, which, sub, rows_per_batch, alpha):
    n, d = h.shape
    tm = min(ROW_TILE, rows_per_batch)
    pick = lambda w: pl.BlockSpec((None, None) + w.shape[2:], lambda i: (layer, which, 0, 0),
                                  pipeline_mode=pl.Buffered(1))
    return pl.pallas_call(
        functools.partial(_ffn_kernel, i0=3 * sub, alpha=alpha),
        grid=(n // tm,),
        in_specs=[
            pl.BlockSpec((tm, d), lambda i: (i, 0)),
            _mod_spec(mod, rows_per_batch // tm),
            _const_spec((1, d)), _const_spec((1, d)),
            pick(w_in), pick(w_out),
        ],
        out_specs=pl.BlockSpec((tm, d), lambda i: (i, 0)),
        out_shape=jax.ShapeDtypeStruct((n, d), F32),
        compiler_params=_params("parallel"),
        name="ffn",
    )(h, mod, ln_g.reshape(1, d), ln_b.reshape(1, d), w_in, w_out)


def _rope_tables(seq):
    t = jnp.arange(seq, dtype=F32)
    row = jnp.floor(t / GRID_W)
    col = t - row * GRID_W
    inv_freq = ROPE_THETA ** (-jnp.arange(ROPE_FREQS, dtype=F32) / ROPE_FREQS)
    lane = jnp.arange(LANES)
    seg = lane % DIFF_HEAD_DIM
    pos = jnp.where((seg // (2 * ROPE_FREQS))[None, :] == 0, row[:, None], col[:, None])
    ang = pos * inv_freq[seg % ROPE_FREQS][None, :]
    sign = jnp.where((seg % (2 * ROPE_FREQS)) < ROPE_FREQS, -1.0, 1.0).astype(F32)
    return jnp.cos(ang), jnp.sin(ang) * sign[None, :]


def _mix_in_kernel(h_ref, mod_ref, w_ref, cos_ref, sin_ref, q_ref, k_ref, vt_ref, u_ref):
    x = _modulate(h_ref[...], mod_ref, 3).astype(BF16)
    z = _dot(x, w_ref[...])
    cos = cos_ref[...]
    sin = sin_ref[...]
    first_half = (lax.broadcasted_iota(jnp.int32, cos.shape, 1) % (2 * ROPE_FREQS)) < ROPE_FREQS

    def rope(blk):
        swapped = jnp.where(first_half, pltpu.roll(blk, LANES - ROPE_FREQS, 1), pltpu.roll(blk, ROPE_FREQS, 1))
        return blk * cos + swapped * sin

    q_scale = DIFF_HEAD_DIM ** -0.5 * math.log2(math.e)
    for j in range(DIFF_WIDTH // LANES):
        sl = slice(j * LANES, (j + 1) * LANES)
        q_ref[:, sl] = (rope(z[:, sl]) * q_scale).astype(BF16)
        k_ref[:, sl] = rope(z[:, DIFF_WIDTH + j * LANES:DIFF_WIDTH + (j + 1) * LANES]).astype(BF16)
    vt_ref[0] = z[:, 2 * DIFF_WIDTH:3 * DIFF_WIDTH].T.astype(BF16)
    u_ref[...] = z[:, 3 * DIFF_WIDTH:]


def _mix_in(h, mod, w, cos, sin, *, seq):
    n, d = h.shape
    tm = min(ROW_TILE, seq)
    tpb = seq // tm
    row_spec = lambda width: pl.BlockSpec((tm, width), lambda i: (i, 0))
    tab_spec = pl.BlockSpec((tm, LANES), lambda i: (i % tpb, 0))
    vt_spec = pl.BlockSpec((1, DIFF_WIDTH, tm), lambda i: (i // tpb, 0, i % tpb))
    return pl.pallas_call(
        _mix_in_kernel,
        grid=(n // tm,),
        in_specs=[row_spec(d), _mod_spec(mod, tpb), _const_spec(w.shape), tab_spec, tab_spec],
        out_specs=[row_spec(DIFF_WIDTH), row_spec(DIFF_WIDTH), vt_spec, row_spec(POOL_WIDTH)],
        out_shape=[jax.ShapeDtypeStruct((n, DIFF_WIDTH), BF16), jax.ShapeDtypeStruct((n, DIFF_WIDTH), BF16),
                   jax.ShapeDtypeStruct((n // seq, DIFF_WIDTH, seq), BF16), jax.ShapeDtypeStruct((n, POOL_WIDTH), F32)],
        compiler_params=_params("parallel"),
        name="mix_in",
    )(h, mod, w, cos, sin)


def _ctx_kv_kernel(h_ref, mod_ref, w_ref, k_ref, vt_ref):
    x = _modulate(h_ref[...], mod_ref, 3).astype(BF16)
    z = _dot(x, w_ref[:, DIFF_WIDTH:3 * DIFF_WIDTH])
    k_ref[...] = z[:, :DIFF_WIDTH].astype(BF16)
    vt_ref[0] = z[:, DIFF_WIDTH:].T.astype(BF16)


def _ctx_kv(h, mod, w, *, seq):
    n, d = h.shape
    tm = min(ROW_TILE, seq)
    tpb = seq // tm
    return pl.pallas_call(
        _ctx_kv_kernel,
        grid=(n // tm,),
        in_specs=[pl.BlockSpec((tm, d), lambda i: (i, 0)), _mod_spec(mod, tpb), _const_spec(w.shape)],
        out_specs=[pl.BlockSpec((tm, DIFF_WIDTH), lambda i: (i, 0)),
                   pl.BlockSpec((1, DIFF_WIDTH, tm), lambda i: (i // tpb, 0, i % tpb))],
        out_shape=[jax.ShapeDtypeStruct((n, DIFF_WIDTH), BF16), jax.ShapeDtypeStruct((n // seq, DIFF_WIDTH, seq), BF16)],
        compiler_params=_params("parallel"),
        name="ctx_kv",
    )(h, mod, w)


def _attn_kernel(lq1_ref, lk1_ref, lq2_ref, lk2_ref, g_ref, q_ref, kl_ref, vtl_ref, kc_ref, vtc_ref, o_ref,
                 s_scr, e_scr, *, lam_init):
    t, s_ctx = kl_ref.shape[1], kc_ref.shape[1]
    lam = (jnp.exp(jnp.sum(lq1_ref[...] * lk1_ref[...], axis=-1, keepdims=True))
           - jnp.exp(jnp.sum(lq2_ref[...] * lk2_ref[...], axis=-1, keepdims=True)) + lam_init)
    lane = lax.broadcasted_iota(jnp.int32, (1, DIFF_V_DIM), 1)
    out_gain = g_ref[...] * (1.0 - lam_init)
    units = [(hd, mp) for hd in range(N_DIFF_HEADS) for mp in range(2)]
    chunks = [(kc_ref, vtc_ref, 0, 0, s_ctx)]
    chunks += [(kl_ref, vtl_ref, c * ATTN_KEY_CHUNK, s_ctx + c * ATTN_KEY_CHUNK, ATTN_KEY_CHUNK)
               for c in range(t // ATTN_KEY_CHUNK)]
    col_max, col_sum, acc, norm = {}, {}, {}, {}

    def scores(ui, ci):
        hd, mp = units[ui]
        cols = slice(hd * DIFF_V_DIM, (hd + 1) * DIFF_V_DIM)
        k_ref, _, off, soff, n = chunks[ci]
        q = jnp.where((lane < DIFF_HEAD_DIM) if mp == 0 else (lane >= DIFF_HEAD_DIM), q_ref[0, :, cols], 0)
        s = _dot_nt(k_ref[0, off:off + n, cols], q)
        s_scr[ui % 2, soff:soff + n, :] = s
        m = jnp.max(s, axis=0, keepdims=True)
        col_max[ui] = m if ci == 0 else jnp.maximum(col_max[ui], m)

    def exponentials(ui, ci):
        _, _, _, soff, n = chunks[ci]
        e = jnp.exp2(s_scr[ui % 2, soff:soff + n, :] - col_max[ui])
        e_scr[ui % 4, soff:soff + n, :] = e.astype(BF16)
        l = jnp.sum(e, axis=0, keepdims=True)
        col_sum[ui] = l if ci == 0 else col_sum[ui] + l

    def values(hd, ci):
        rows = slice(hd * DIFF_V_DIM, (hd + 1) * DIFF_V_DIM)
        _, vt_ref, off, soff, n = chunks[ci]
        if ci == 0:
            l1, l2 = col_sum.pop(2 * hd), col_sum.pop(2 * hd + 1)
            norm[hd] = ((lam * l1 / l2).astype(BF16), l1)
        p = e_scr[(2 * hd) % 4, soff:soff + n, :] - e_scr[(2 * hd + 1) % 4, soff:soff + n, :] * norm[hd][0]
        d = _dot(vt_ref[0, rows, off:off + n], p)
        acc[hd] = d if ci == 0 else acc[hd] + d
        if ci == len(chunks) - 1:
            o = acc.pop(hd) / norm.pop(hd)[1]
            o = o * lax.rsqrt(jnp.mean(o * o, axis=0, keepdims=True) + LN_EPS) * out_gain
            o_ref[0, :, rows] = o.T.astype(BF16)

    for r in range(len(units) + 2):
        for ci in range(len(chunks)):
            if r < len(units):
                scores(r, ci)
            if 1 <= r <= len(units):
                exponentials(r - 1, ci)
            if r >= 3 and r % 2 == 1:
                values((r - 3) // 2, ci)


def _attention(q, k_lat, vt_lat, k_ctx, vt_ctx, lq1, lk1, lq2, lk2, subln_g, *, lam_init):
    b, t, w = q.shape
    s_ctx = k_ctx.shape[1]
    tq = min(ATTN_Q_TILE, t)
    assert t % ATTN_KEY_CHUNK == 0
    vec = lambda a: a.reshape(1, -1).astype(F32)
    lam_spec = pl.BlockSpec((1, DIFF_HEAD_DIM), lambda bi, qi: (0, 0))
    return pl.pallas_call(
        functools.partial(_attn_kernel, lam_init=lam_init),
        grid=(b, t // tq),
        in_specs=[
            lam_spec, lam_spec, lam_spec, lam_spec,
            pl.BlockSpec((DIFF_V_DIM, 1), lambda bi, qi: (0, 0)),
            pl.BlockSpec((1, tq, w), lambda bi, qi: (bi, qi, 0)),
            pl.BlockSpec((1, t, w), lambda bi, qi: (bi, 0, 0)),
            pl.BlockSpec((1, w, t), lambda bi, qi: (bi, 0, 0)),
            pl.BlockSpec((1, s_ctx, w), lambda bi, qi: (bi, 0, 0)),
            pl.BlockSpec((1, w, s_ctx), lambda bi, qi: (bi, 0, 0)),
        ],
        out_specs=pl.BlockSpec((1, tq, w), lambda bi, qi: (bi, qi, 0)),
        out_shape=jax.ShapeDtypeStruct((b, t, w), BF16),
        scratch_shapes=[pltpu.VMEM((2, t + s_ctx, tq), F32), pltpu.VMEM((4, t + s_ctx, tq), BF16)],
        compiler_params=_params("parallel", "arbitrary"),
        name="diff_attn",
    )(vec(lq1), vec(lk1), vec(lq2), vec(lk2), subln_g.reshape(DIFF_V_DIM, 1).astype(F32),
      q, k_lat, vt_lat, k_ctx, vt_ctx)


def _mix_out_kernel(h_ref, mod_ref, lng_ref, lnb_ref, o_ref, u_ref, up_ref, un_ref, wpool_ref, pscale_ref, wout_ref,
                    out_ref, ext_ref, *, seq, alpha):
    tm = h_ref.shape[0]
    tpb = seq // tm
    ib = pl.program_id(0) % tpb
    ext_ref[0:POOL_HALO, :] = jnp.where(ib > 0, up_ref[...], 0.0)
    ext_ref[POOL_HALO:POOL_HALO + tm, :] = u_ref[...]
    ext_ref[POOL_HALO + tm:, :] = jnp.where(ib < tpb - 1, un_ref[...], 0.0)
    pos = ib * tm + lax.broadcasted_iota(jnp.int32, (tm, 1), 0)

    y = _dot(o_ref[...], wout_ref[:DIFF_WIDTH, :])
    for g, win in enumerate(POOL_WINDOWS):
        sl = slice(g * POOL_GROUP_DIM, (g + 1) * POOL_GROUP_DIM)
        acc = ext_ref[POOL_HALO - win // 2:POOL_HALO - win // 2 + tm, sl]
        for d in range(1 - win // 2, win // 2):
            acc = acc + ext_ref[POOL_HALO + d:POOL_HALO + d + tm, sl]
        cnt = jnp.minimum(pos + win // 2, seq) - jnp.maximum(pos - win // 2, 0)
        pooled = acc / cnt.astype(F32) - u_ref[:, sl]
        yg = _dot(pooled.astype(BF16), wpool_ref[g]) * pscale_ref[:, sl]
        y = y + _dot(yg.astype(BF16), wout_ref[DIFF_WIDTH + g * POOL_GROUP_DIM:DIFF_WIDTH + (g + 1) * POOL_GROUP_DIM, :])
    r = alpha * h_ref[...] + mod_ref[0, 5:6, :] * y
    out_ref[...] = _layer_norm(r, lng_ref[...], lnb_ref[...])


def _mix_out(h, mod, ln_g, ln_b, o, u, w_pool, pool_scale, w_out, *, seq, alpha):
    n, d = h.shape
    tm = min(ROW_TILE, seq)
    hb = tm // POOL_HALO
    last = n // POOL_HALO - 1
    row_spec = lambda width: pl.BlockSpec((tm, width), lambda i: (i, 0))
    return pl.pallas_call(
        functools.partial(_mix_out_kernel, seq=seq, alpha=alpha),
        grid=(n // tm,),
        in_specs=[
            row_spec(d), _mod_spec(mod, seq // tm), _const_spec((1, d)), _const_spec((1, d)),
            row_spec(DIFF_WIDTH), row_spec(POOL_WIDTH),
            pl.BlockSpec((POOL_HALO, POOL_WIDTH), lambda i: (jnp.maximum(i * hb - 1, 0), 0)),
            pl.BlockSpec((POOL_HALO, POOL_WIDTH), lambda i: (jnp.minimum((i + 1) * hb, last), 0)),
            _const_spec(w_pool.shape), _const_spec((1, POOL_WIDTH)), _const_spec(w_out.shape),
        ],
        out_specs=row_spec(d),
        out_shape=jax.ShapeDtypeStruct((n, d), F32),
        scratch_shapes=[pltpu.VMEM((tm + 2 * POOL_HALO, POOL_WIDTH), F32)],
        compiler_params=_params("parallel"),
        name="mix_out",
    )(h, mod, ln_g.reshape(1, d), ln_b.reshape(1, d), o, u, u, u, w_pool, pool_scale.reshape(1, -1), w_out)


def _conv_kernel(h_ref, hp_ref, hn_ref, mod_ref, lng_ref, lnb_ref, wc1_ref, bc1_ref, wdw_ref, bdw_ref,
                 cg_ref, cb_ref, wc2_ref, bc2_ref, out_ref, x_ref, z_ref, acc_ref, *, seq, alpha):
    tm, d = h_ref.shape
    tpb = seq // tm
    ib = pl.program_id(0) % tpb
    rows = tm + 2 * CONV_HALO

    x_ref[0:CONV_HALO, :] = _modulate(hp_ref[...], mod_ref, 3).astype(BF16)
    x_ref[CONV_HALO:CONV_HALO + tm, :] = _modulate(h_ref[...], mod_ref, 3).astype(BF16)
    x_ref[CONV_HALO + tm:, :] = _modulate(hn_ref[...], mod_ref, 3).astype(BF16)
    ridx = lax.broadcasted_iota(jnp.int32, (rows, 1), 0)
    inside = jnp.logical_and(jnp.logical_or(ridx >= CONV_HALO, ib > 0),
                             jnp.logical_or(ridx < CONV_HALO + tm, ib < tpb - 1))

    first = CONV_HALO - CONV_WIDTH // 2
    span = CONV_ROW_CHUNK + SUBLANES

    def glu_block(j):
        ca = slice(j * CONV_CH_BLOCK, (j + 1) * CONV_CH_BLOCK)
        cg = slice(d + j * CONV_CH_BLOCK, d + (j + 1) * CONV_CH_BLOCK)
        x = x_ref[...]
        a = _dot(x, wc1_ref[:, ca]) + bc1_ref[:, ca]
        g = _dot(x, wc1_ref[:, cg]) + bc1_ref[:, cg]
        z_ref[:, ca] = jnp.where(inside, a * jax.nn.sigmoid(g), 0.0)

    def conv_block(j):
        for c in range(j * CONV_CH_BLOCK // LANES, (j + 1) * CONV_CH_BLOCK // LANES):
            cols = slice(c * LANES, (c + 1) * LANES)
            for r0 in range(0, tm, CONV_ROW_CHUNK):
                acc = jnp.broadcast_to(bdw_ref[:, cols], (CONV_ROW_CHUNK, LANES))
                for b in range(SUBLANES):
                    part = None
                    for a in range(-(-(first + CONV_WIDTH) // SUBLANES)):
                        k = SUBLANES * a + b - first
                        if 0 <= k < CONV_WIDTH:
                            term = z_ref[r0 + SUBLANES * a:r0 + SUBLANES * a + span, cols] * wdw_ref[k:k + 1, cols]
                            part = term if part is None else part + term
                    acc = acc + part[b:b + CONV_ROW_CHUNK]
                acc_ref[r0:r0 + CONV_ROW_CHUNK, cols] = acc

    n_blocks = d // CONV_CH_BLOCK
    glu_block(0)
    for j in range(n_blocks):
        if j + 1 < n_blocks:
            glu_block(j + 1)
        conv_block(j)

    zc = jax.nn.silu(_layer_norm(acc_ref[...], cg_ref[...], cb_ref[...])).astype(BF16)
    y = _dot(zc, wc2_ref[...]) + bc2_ref[...]
    r = alpha * h_ref[...] + mod_ref[0, 5:6, :] * y
    out_ref[...] = _layer_norm(r, lng_ref[...], lnb_ref[...])


def _conv(h, mod, ln_g, ln_b, w_c1, b_c1, w_dw, b_dw, cg, cb, w_c2, b_c2, *, seq, alpha):
    n, d = h.shape
    tm = min(ROW_TILE, seq)
    hb = tm // CONV_HALO
    last = n // CONV_HALO - 1
    row = lambda a: a.reshape(1, -1)
    return pl.pallas_call(
        functools.partial(_conv_kernel, seq=seq, alpha=alpha),
        grid=(n // tm,),
        in_specs=[
            pl.BlockSpec((tm, d), lambda i: (i, 0)),
            pl.BlockSpec((CONV_HALO, d), lambda i: (jnp.maximum(i * hb - 1, 0), 0)),
            pl.BlockSpec((CONV_HALO, d), lambda i: (jnp.minimum((i + 1) * hb, last), 0)),
            _mod_spec(mod, seq // tm), _const_spec((1, d)), _const_spec((1, d)),
            _const_spec(w_c1.shape), _const_spec((1, 2 * d)), _const_spec(w_dw.shape), _const_spec((1, d)),
            _const_spec((1, d)), _const_spec((1, d)), _const_spec(w_c2.shape), _const_spec((1, d)),
        ],
        out_specs=pl.BlockSpec((tm, d), lambda i: (i, 0)),
        out_shape=jax.ShapeDtypeStruct((n, d), F32),
        scratch_shapes=[pltpu.VMEM((tm + 2 * CONV_HALO, d), BF16), pltpu.VMEM((tm + 2 * CONV_HALO, d), F32),
                        pltpu.VMEM((tm, d), F32)],
        compiler_params=_params("parallel"),
        name="conv",
    )(h, h, h, mod, row(ln_g), row(ln_b), w_c1, row(b_c1), w_dw, row(b_dw), row(cg), row(cb), w_c2, row(b_c2))


def kernel(x, c, ctx, c_ctx, w_ada, b_ada, ln_g, ln_b, w_ffn_in, w_ffn_out, w_mix_in, w_mix_out,
           lam_q1, lam_k1, lam_q2, lam_k2, subln_g, w_pool, pool_scale,
           w_c1, b_c1, w_dw, b_dw, conv_ln_g, conv_ln_b, w_c2, b_c2):
    batch, seq, d = x.shape
    ctx_len = ctx.shape[1]
    depth = w_ada.shape[0]
    alpha = (2.0 * depth) ** 0.25
    assert d == D_MODEL and seq % GRID_W == 0
    assert seq % min(ROW_TILE, seq) == 0 and ctx_len % min(ROW_TILE, ctx_len) == 0 and seq % min(ATTN_Q_TILE, seq) == 0

    n_rows = -(-(batch + 1) // SUBLANES) * SUBLANES
    cc = jnp.zeros((n_rows, d), F32).at[:batch].set(c).at[batch].set(c_ctx)
    mods = _ada(cc, w_ada, b_ada)
    cos, sin = _rope_tables(seq)

    bf = lambda w: w.astype(BF16)
    w_ffn_in, w_ffn_out, w_mix_in, w_mix_out, w_pool = bf(w_ffn_in), bf(w_ffn_out), bf(w_mix_in), bf(w_mix_out), bf(w_pool)
    w_c1, w_c2 = bf(w_c1), bf(w_c2)

    h_lat = x.reshape(batch * seq, d)
    h_ctx = ctx.reshape(batch * ctx_len, d)
    for l in range(depth):
        even = l % 2 == 0
        ctx_later = any(j % 2 == 0 for j in range(l + 1, depth))
        m_lat = mods[l, :batch].reshape(batch, N_MOD, d)
        m_ctx = mods[l, batch].reshape(1, N_MOD, d)
        ffn = lambda h, m, sub, j, rpb: _ffn(h, m, ln_g[l, sub], ln_b[l, sub], w_ffn_in, w_ffn_out,
                                             layer=l, which=j, sub=sub, rows_per_batch=rpb, alpha=alpha)

        h_lat = ffn(h_lat, m_lat, 0, 0, seq)
        if even or ctx_later:
            h_ctx = ffn(h_ctx, m_ctx, 0, 0, ctx_len)

        if even:
            e = l // 2
            lam_init = 0.8 - 0.6 * math.exp(-0.3 * l)
            q, k_lat, vt_lat, u = _mix_in(h_lat, m_lat, w_mix_in[e], cos, sin, seq=seq)
            if ctx_later:
                raise NotImplementedError("context outputs of an even mixer are not needed at this depth")
            k_ctx, vt_ctx = _ctx_kv(h_ctx, m_ctx, w_mix_in[e], seq=ctx_len)
            lat3 = lambda a: a.reshape(batch, seq, DIFF_WIDTH)
            o = _attention(lat3(q), lat3(k_lat), vt_lat, k_ctx.reshape(batch, ctx_len, DIFF_WIDTH), vt_ctx,
                           lam_q1[e], lam_k1[e], lam_q2[e], lam_k2[e], subln_g[e], lam_init=lam_init)
            h_lat = _mix_out(h_lat, m_lat, ln_g[l, 1], ln_b[l, 1], o.reshape(batch * seq, DIFF_WIDTH), u,
                             w_pool[e], pool_scale[e], w_mix_out[e], seq=seq, alpha=alpha)
        else:
            o = l // 2
            if ctx_later:
                raise NotImplementedError("context outputs of an odd mixer are not needed at this depth")
            h_lat = _conv(h_lat, m_lat, ln_g[l, 1], ln_b[l, 1], w_c1[o], b_c1[o], w_dw[o], b_dw[o],
                          conv_ln_g[o], conv_ln_b[o], w_c2[o], b_c2[o], seq=seq, alpha=alpha)

        h_lat = ffn(h_lat, m_lat, 2, 1, seq)
        if ctx_later:
            h_ctx = ffn(h_ctx, m_ctx, 2, 1, ctx_len)
    return h_lat.reshape(batch, seq, d)
```

```python
import functools
import math

import jax
import jax.numpy as jnp
from jax import lax
from jax.experimental import pallas as pl
from jax.experimental.pallas import tpu as pltpu

D_MODEL = 1024
GRID_W = 64
N_DIFF_HEADS = 4
DIFF_HEAD_DIM = 64
DIFF_V_DIM = 2 * DIFF_HEAD_DIM
DIFF_WIDTH = N_DIFF_HEADS * DIFF_V_DIM
ROPE_FREQS = DIFF_HEAD_DIM // 4
ROPE_THETA = 10000.0
POOL_WINDOWS = (2, 4, 8, 16)
POOL_GROUP_DIM = D_MODEL // 8
POOL_WIDTH = len(POOL_WINDOWS) * POOL_GROUP_DIM
CONV_WIDTH = 31
D_FF = 2816
N_MOD = 9
FFN_HALF = 0.5
LN_EPS = 1e-5

F32 = jnp.float32
BF16 = jnp.bfloat16

SUBLANES = 8
LANES = 128
VMEM_LIMIT_BYTES = 56 * 1024 * 1024

ROW_TILE = 512
FFN_ROW_TILE = 1024
FFN_GROUP_ROWS = 256
ATTN_Q_TILE = 256
ATTN_KEY_CHUNK = 512
ADA_COL_TILE = 1152
POOL_HALO = 8
CONV_HALO = 16
CONV_ROW_CHUNK = 64
CONV_CH_BLOCK = 256


def _params(*sem):
    return pltpu.CompilerParams(dimension_semantics=sem, vmem_limit_bytes=VMEM_LIMIT_BYTES)


def _const_spec(shape):
    nd = len(shape)
    return pl.BlockSpec(shape, lambda *_: (0,) * nd, pipeline_mode=pl.Buffered(1))


def _layer_norm(r, g, b):
    mu = jnp.mean(r, axis=-1, keepdims=True)
    d = r - mu
    var = jnp.mean(d * d, axis=-1, keepdims=True)
    return d * lax.rsqrt(var + LN_EPS) * g + b


def _dot(a, b):
    return jnp.dot(a, b, preferred_element_type=F32)


def _dot_nt(a, b):
    return lax.dot_general(a, b, (((1,), (1,)), ((), ())), preferred_element_type=F32)


def _ada_kernel(c_ref, w_ref, b_ref, o_ref):
    s = jax.nn.silu(c_ref[...]).astype(BF16)
    o_ref[0] = _dot(s, w_ref[0].astype(BF16)) + b_ref[0]


def _ada(cc, w_ada, b_ada):
    depth, d, n = w_ada.shape
    r = cc.shape[0]
    return pl.pallas_call(
        _ada_kernel,
        grid=(depth, n // ADA_COL_TILE),
        in_specs=[
            pl.BlockSpec((r, d), lambda l, j: (0, 0)),
            pl.BlockSpec((1, d, ADA_COL_TILE), lambda l, j: (l, 0, j)),
            pl.BlockSpec((1, 1, ADA_COL_TILE), lambda l, j: (l, 0, j)),
        ],
        out_specs=pl.BlockSpec((1, r, ADA_COL_TILE), lambda l, j: (l, 0, j)),
        out_shape=jax.ShapeDtypeStruct((depth, r, n), F32),
        compiler_params=_params("arbitrary", "arbitrary"),
        name="ada",
    )(cc, w_ada, b_ada.reshape(depth, 1, n))


def _mod_spec(mod, tiles_per_batch):
    if mod.shape[0] == 1:
        return pl.BlockSpec((1, N_MOD, D_MODEL), lambda i: (0, 0, 0))
    return pl.BlockSpec((1, N_MOD, D_MODEL), lambda i: (i // tiles_per_batch, 0, 0))


def _modulate(h, mod_ref, i0):
    return h * (1.0 + mod_ref[0, i0 + 1:i0 + 2, :]) + mod_ref[0, i0:i0 + 1, :]


def _ffn_kernel(h_ref, mod_ref, lng_ref, lnb_ref, win_ref, wout_ref, o_ref, *, i0, alpha, parts):
    tm = h_ref.shape[0]
    groups = [slice(p * tm // parts, (p + 1) * tm // parts) for p in range(parts)]
    hs = [h_ref[rows, :] for rows in groups]
    gate_up = []
    for h in hs:
        x = _modulate(h, mod_ref, i0).astype(BF16)
        gate_up.append((_dot(x, win_ref[:, :D_FF]), _dot(x, win_ref[:, D_FF:])))
    ys = [_dot((jax.nn.silu(g) * u).astype(BF16), wout_ref[...]) for g, u in gate_up]
    for rows, h, y in zip(groups, hs, ys):
        r = alpha * h + (FFN_HALF * mod_ref[0, i0 + 2:i0 + 3, :]) * y
        o_ref[rows, :] = _layer_norm(r, lng_ref[...], lnb_ref[...])


def _ffn(h, mod, ln_g, ln_b, w_in, w_out, *, layer, which, sub, rows_per_batch, alpha):
    n, d = h.shape
    tm = min(FFN_ROW_TILE, rows_per_batch)
    pick = lambda w: pl.BlockSpec((None, None) + w.shape[2:], lambda i: (layer, which, 0, 0),
                                  pipeline_mode=pl.Buffered(1))
    return pl.pallas_call(
        functools.partial(_ffn_kernel, i0=3 * sub, alpha=alpha, parts=max(1, tm // FFN_GROUP_ROWS)),
        grid=(n // tm,),
        in_specs=[
            pl.BlockSpec((tm, d), lambda i: (i, 0)),
            _mod_spec(mod, rows_per_batch // tm),
            _const_spec((1, d)), _const_spec((1, d)),
            pick(w_in), pick(w_out),
        ],
        out_specs=pl.BlockSpec((tm, d), lambda i: (i, 0)),
        out_shape=jax.ShapeDtypeStruct((n, d), F32),
        compiler_params=_params("parallel"),
        name="ffn",
    )(h, mod, ln_g.reshape(1, d), ln_b.reshape(1, d), w_in, w_out)


def _rope_tables(seq):
    t = jnp.arange(seq, dtype=F32)
    row = jnp.floor(t / GRID_W)
    col = t - row * GRID_W
    inv_freq = ROPE_THETA ** (-jnp.arange(ROPE_FREQS, dtype=F32) / ROPE_FREQS)
    lane = jnp.arange(LANES)
    seg = lane % DIFF_HEAD_DIM
    pos = jnp.where((seg // (2 * ROPE_FREQS))[None, :] == 0, row[:, None], col[:, None])
    ang = pos * inv_freq[seg % ROPE_FREQS][None, :]
    sign = jnp.where((seg % (2 * ROPE_FREQS)) < ROPE_FREQS, -1.0, 1.0).astype(F32)
    return jnp.cos(ang), jnp.sin(ang) * sign[None, :]


def _mix_in_kernel(h_ref, mod_ref, w_ref, cos_ref, sin_ref, q_ref, k_ref, vt_ref, u_ref):
    x = _modulate(h_ref[...], mod_ref, 3).astype(BF16)
    z = _dot(x, w_ref[...])
    cos = cos_ref[...]
    sin = sin_ref[...]
    first_half = (lax.broadcasted_iota(jnp.int32, cos.shape, 1) % (2 * ROPE_FREQS)) < ROPE_FREQS

    def rope(blk):
        swapped = jnp.where(first_half, pltpu.roll(blk, LANES - ROPE_FREQS, 1), pltpu.roll(blk, ROPE_FREQS, 1))
        return blk * cos + swapped * sin

    q_scale = DIFF_HEAD_DIM ** -0.5 * math.log2(math.e)
    for j in range(DIFF_WIDTH // LANES):
        sl = slice(j * LANES, (j + 1) * LANES)
        q_ref[:, sl] = (rope(z[:, sl]) * q_scale).astype(BF16)
        k_ref[:, sl] = rope(z[:, DIFF_WIDTH + j * LANES:DIFF_WIDTH + (j + 1) * LANES]).astype(BF16)
    vt_ref[0] = z[:, 2 * DIFF_WIDTH:3 * DIFF_WIDTH].T.astype(BF16)
    u_ref[...] = z[:, 3 * DIFF_WIDTH:]


def _mix_in(h, mod, w, cos, sin, *, seq):
    n, d = h.shape
    tm = min(ROW_TILE, seq)
    tpb = seq // tm
    row_spec = lambda width: pl.BlockSpec((tm, width), lambda i: (i, 0))
    tab_spec = pl.BlockSpec((tm, LANES), lambda i: (i % tpb, 0))
    vt_spec = pl.BlockSpec((1, DIFF_WIDTH, tm), lambda i: (i // tpb, 0, i % tpb))
    return pl.pallas_call(
        _mix_in_kernel,
        grid=(n // tm,),
        in_specs=[row_spec(d), _mod_spec(mod, tpb), _const_spec(w.shape), tab_spec, tab_spec],
        out_specs=[row_spec(DIFF_WIDTH), row_spec(DIFF_WIDTH), vt_spec, row_spec(POOL_WIDTH)],
        out_shape=[jax.ShapeDtypeStruct((n, DIFF_WIDTH), BF16), jax.ShapeDtypeStruct((n, DIFF_WIDTH), BF16),
                   jax.ShapeDtypeStruct((n // seq, DIFF_WIDTH, seq), BF16), jax.ShapeDtypeStruct((n, POOL_WIDTH), F32)],
        compiler_params=_params("parallel"),
        name="mix_in",
    )(h, mod, w, cos, sin)


def _ctx_kv_kernel(h_ref, mod_ref, w_ref, k_ref, vt_ref):
    x = _modulate(h_ref[...], mod_ref, 3).astype(BF16)
    z = _dot(x, w_ref[:, DIFF_WIDTH:3 * DIFF_WIDTH])
    k_ref[...] = z[:, :DIFF_WIDTH].astype(BF16)
    vt_ref[0] = z[:, DIFF_WIDTH:].T.astype(BF16)


def _ctx_kv(h, mod, w, *, seq):
    n, d = h.shape
    tm = min(ROW_TILE, seq)
    tpb = seq // tm
    return pl.pallas_call(
        _ctx_kv_kernel,
        grid=(n // tm,),
        in_specs=[pl.BlockSpec((tm, d), lambda i: (i, 0)), _mod_spec(mod, tpb), _const_spec(w.shape)],
        out_specs=[pl.BlockSpec((tm, DIFF_WIDTH), lambda i: (i, 0)),
                   pl.BlockSpec((1, DIFF_WIDTH, tm), lambda i: (i // tpb, 0, i % tpb))],
        out_shape=[jax.ShapeDtypeStruct((n, DIFF_WIDTH), BF16), jax.ShapeDtypeStruct((n // seq, DIFF_WIDTH, seq), BF16)],
        compiler_params=_params("parallel"),
        name="ctx_kv",
    )(h, mod, w)


def _attn_kernel(lq1_ref, lk1_ref, lq2_ref, lk2_ref, g_ref, q_ref, kl_ref, vtl_ref, kc_ref, vtc_ref, o_ref,
                 s_scr, e_scr, *, lam_init):
    t, s_ctx = kl_ref.shape[1], kc_ref.shape[1]
    lam = (jnp.exp(jnp.sum(lq1_ref[...] * lk1_ref[...], axis=-1, keepdims=True))
           - jnp.exp(jnp.sum(lq2_ref[...] * lk2_ref[...], axis=-1, keepdims=True)) + lam_init)
    lane = lax.broadcasted_iota(jnp.int32, (1, DIFF_V_DIM), 1)
    out_gain = g_ref[...] * (1.0 - lam_init)
    units = [(hd, mp) for hd in range(N_DIFF_HEADS) for mp in range(2)]
    chunks = [(kc_ref, vtc_ref, 0, 0, s_ctx)]
    chunks += [(kl_ref, vtl_ref, c * ATTN_KEY_CHUNK, s_ctx + c * ATTN_KEY_CHUNK, ATTN_KEY_CHUNK)
               for c in range(t // ATTN_KEY_CHUNK)]
    col_max, col_sum, acc, norm = {}, {}, {}, {}

    def scores(ui, ci):
        hd, mp = units[ui]
        cols = slice(hd * DIFF_V_DIM, (hd + 1) * DIFF_V_DIM)
        k_ref, _, off, soff, n = chunks[ci]
        q = jnp.where((lane < DIFF_HEAD_DIM) if mp == 0 else (lane >= DIFF_HEAD_DIM), q_ref[0, :, cols], 0)
        s = _dot_nt(k_ref[0, off:off + n, cols], q)
        s_scr[ui % 2, soff:soff + n, :] = s
        m = jnp.max(s, axis=0, keepdims=True)
        col_max[ui] = m if ci == 0 else jnp.maximum(col_max[ui], m)

    def exponentials(ui, ci):
        _, _, _, soff, n = chunks[ci]
        e = jnp.exp2(s_scr[ui % 2, soff:soff + n, :] - col_max[ui])
        e_scr[ui % 4, soff:soff + n, :] = e.astype(BF16)
        l = jnp.sum(e, axis=0, keepdims=True)
        col_sum[ui] = l if ci == 0 else col_sum[ui] + l

    def values(hd, ci):
        rows = slice(hd * DIFF_V_DIM, (hd + 1) * DIFF_V_DIM)
        _, vt_ref, off, soff, n = chunks[ci]
        if ci == 0:
            l1, l2 = col_sum.pop(2 * hd), col_sum.pop(2 * hd + 1)
            norm[hd] = ((lam * l1 / l2).astype(BF16), l1)
        p = e_scr[(2 * hd) % 4, soff:soff + n, :] - e_scr[(2 * hd + 1) % 4, soff:soff + n, :] * norm[hd][0]
        d = _dot(vt_ref[0, rows, off:off + n], p)
        acc[hd] = d if ci == 0 else acc[hd] + d
        if ci == len(chunks) - 1:
            o = acc.pop(hd) / norm.pop(hd)[1]
            o = o * lax.rsqrt(jnp.mean(o * o, axis=0, keepdims=True) + LN_EPS) * out_gain
            o_ref[0, :, rows] = o.T.astype(BF16)

    for r in range(len(units) + 2):
        for ci in range(len(chunks)):
            if r < len(units):
                scores(r, ci)
            if 1 <= r <= len(units):
                exponentials(r - 1, ci)
            if r >= 3 and r % 2 == 1:
                values((r - 3) // 2, ci)


def _attention(q, k_lat, vt_lat, k_ctx, vt_ctx, lq1, lk1, lq2, lk2, subln_g, *, lam_init):
    b, t, w = q.shape
    s_ctx = k_ctx.shape[1]
    tq = min(ATTN_Q_TILE, t)
    assert t % ATTN_KEY_CHUNK == 0
    vec = lambda a: a.reshape(1, -1).astype(F32)
    lam_spec = pl.BlockSpec((1, DIFF_HEAD_DIM), lambda bi, qi: (0, 0))
    return pl.pallas_call(
        functools.partial(_attn_kernel, lam_init=lam_init),
        grid=(b, t // tq),
        in_specs=[
            lam_spec, lam_spec, lam_spec, lam_spec,
            pl.BlockSpec((DIFF_V_DIM, 1), lambda bi, qi: (0, 0)),
            pl.BlockSpec((1, tq, w), lambda bi, qi: (bi, qi, 0)),
            pl.BlockSpec((1, t, w), lambda bi, qi: (bi, 0, 0)),
            pl.BlockSpec((1, w, t), lambda bi, qi: (bi, 0, 0)),
            pl.BlockSpec((1, s_ctx, w), lambda bi, qi: (bi, 0, 0)),
            pl.BlockSpec((1, w, s_ctx), lambda bi, qi: (bi, 0, 0)),
        ],
        out_specs=pl.BlockSpec((1, tq, w), lambda bi, qi: (bi, qi, 0)),
        out_shape=jax.ShapeDtypeStruct((b, t, w), BF16),
        scratch_shapes=[pltpu.VMEM((2, t + s_ctx, tq), F32), pltpu.VMEM((4, t + s_ctx, tq), BF16)],
        compiler_params=_params("parallel", "arbitrary"),
        name="diff_attn",
    )(vec(lq1), vec(lk1), vec(lq2), vec(lk2), subln_g.reshape(DIFF_V_DIM, 1).astype(F32),
      q, k_lat, vt_lat, k_ctx, vt_ctx)


def _mix_out_kernel(h_ref, mod_ref, lng_ref, lnb_ref, o_ref, u_ref, up_ref, un_ref, wpool_ref, pscale_ref, wout_ref,
                    out_ref, ext_ref, *, seq, alpha):
    tm = h_ref.shape[0]
    tpb = seq // tm
    ib = pl.program_id(0) % tpb
    ext_ref[0:POOL_HALO, :] = jnp.where(ib > 0, up_ref[...], 0.0)
    ext_ref[POOL_HALO:POOL_HALO + tm, :] = u_ref[...]
    ext_ref[POOL_HALO + tm:, :] = jnp.where(ib < tpb - 1, un_ref[...], 0.0)
    pos = ib * tm + lax.broadcasted_iota(jnp.int32, (tm, 1), 0)

    y = _dot(o_ref[...], wout_ref[:DIFF_WIDTH, :])
    for g, win in enumerate(POOL_WINDOWS):
        sl = slice(g * POOL_GROUP_DIM, (g + 1) * POOL_GROUP_DIM)
        acc = ext_ref[POOL_HALO - win // 2:POOL_HALO - win // 2 + tm, sl]
        for d in range(1 - win // 2, win // 2):
            acc = acc + ext_ref[POOL_HALO + d:POOL_HALO + d + tm, sl]
        cnt = jnp.minimum(pos + win // 2, seq) - jnp.maximum(pos - win // 2, 0)
        pooled = acc / cnt.astype(F32) - u_ref[:, sl]
        yg = _dot(pooled.astype(BF16), wpool_ref[g]) * pscale_ref[:, sl]
        y = y + _dot(yg.astype(BF16), wout_ref[DIFF_WIDTH + g * POOL_GROUP_DIM:DIFF_WIDTH + (g + 1) * POOL_GROUP_DIM, :])
    r = alpha * h_ref[...] + mod_ref[0, 5:6, :] * y
    out_ref[...] = _layer_norm(r, lng_ref[...], lnb_ref[...])


def _mix_out(h, mod, ln_g, ln_b, o, u, w_pool, pool_scale, w_out, *, seq, alpha):
    n, d = h.shape
    tm = min(ROW_TILE, seq)
    hb = tm // POOL_HALO
    last = n // POOL_HALO - 1
    row_spec = lambda width: pl.BlockSpec((tm, width), lambda i: (i, 0))
    return pl.pallas_call(
        functools.partial(_mix_out_kernel, seq=seq, alpha=alpha),
        grid=(n // tm,),
        in_specs=[
            row_spec(d), _mod_spec(mod, seq // tm), _const_spec((1, d)), _const_spec((1, d)),
            row_spec(DIFF_WIDTH), row_spec(POOL_WIDTH),
            pl.BlockSpec((POOL_HALO, POOL_WIDTH), lambda i: (jnp.maximum(i * hb - 1, 0), 0)),
            pl.BlockSpec((POOL_HALO, POOL_WIDTH), lambda i: (jnp.minimum((i + 1) * hb, last), 0)),
            _const_spec(w_pool.shape), _const_spec((1, POOL_WIDTH)), _const_spec(w_out.shape),
        ],
        out_specs=row_spec(d),
        out_shape=jax.ShapeDtypeStruct((n, d), F32),
        scratch_shapes=[pltpu.VMEM((tm + 2 * POOL_HALO, POOL_WIDTH), F32)],
        compiler_params=_params("parallel"),
        name="mix_out",
    )(h, mod, ln_g.reshape(1, d), ln_b.reshape(1, d), o, u, u, u, w_pool, pool_scale.reshape(1, -1), w_out)


def _conv_kernel(h_ref, hp_ref, hn_ref, mod_ref, lng_ref, lnb_ref, wc1_ref, bc1_ref, wdw_ref, bdw_ref,
                 cg_ref, cb_ref, wc2_ref, bc2_ref, out_ref, x_ref, z_ref, acc_ref, *, seq, alpha):
    tm, d = h_ref.shape
    tpb = seq // tm
    ib = pl.program_id(0) % tpb
    rows = tm + 2 * CONV_HALO

    x_ref[0:CONV_HALO, :] = _modulate(hp_ref[...], mod_ref, 3).astype(BF16)
    x_ref[CONV_HALO:CONV_HALO + tm, :] = _modulate(h_ref[...], mod_ref, 3).astype(BF16)
    x_ref[CONV_HALO + tm:, :] = _modulate(hn_ref[...], mod_ref, 3).astype(BF16)
    ridx = lax.broadcasted_iota(jnp.int32, (rows, 1), 0)
    inside = jnp.logical_and(jnp.logical_or(ridx >= CONV_HALO, ib > 0),
                             jnp.logical_or(ridx < CONV_HALO + tm, ib < tpb - 1))

    first = CONV_HALO - CONV_WIDTH // 2
    span = CONV_ROW_CHUNK + SUBLANES

    def glu_block(j):
        ca = slice(j * CONV_CH_BLOCK, (j + 1) * CONV_CH_BLOCK)
        cg = slice(d + j * CONV_CH_BLOCK, d + (j + 1) * CONV_CH_BLOCK)
        x = x_ref[...]
        a = _dot(x, wc1_ref[:, ca]) + bc1_ref[:, ca]
        g = _dot(x, wc1_ref[:, cg]) + bc1_ref[:, cg]
        z_ref[:, ca] = jnp.where(inside, a * jax.nn.sigmoid(g), 0.0)

    def conv_block(j):
        for c in range(j * CONV_CH_BLOCK // LANES, (j + 1) * CONV_CH_BLOCK // LANES):
            cols = slice(c * LANES, (c + 1) * LANES)
            for r0 in range(0, tm, CONV_ROW_CHUNK):
                acc = jnp.broadcast_to(bdw_ref[:, cols], (CONV_ROW_CHUNK, LANES))
                for b in range(SUBLANES):
                    part = None
                    for a in range(-(-(first + CONV_WIDTH) // SUBLANES)):
                        k = SUBLANES * a + b - first
                        if 0 <= k < CONV_WIDTH:
                            term = z_ref[r0 + SUBLANES * a:r0 + SUBLANES * a + span, cols] * wdw_ref[k:k + 1, cols]
                            part = term if part is None else part + term
                    acc = acc + part[b:b + CONV_ROW_CHUNK]
                acc_ref[r0:r0 + CONV_ROW_CHUNK, cols] = acc

    n_blocks = d // CONV_CH_BLOCK
    glu_block(0)
    for j in range(n_blocks):
        if j + 1 < n_blocks:
            glu_block(j + 1)
        conv_block(j)

    zc = jax.nn.silu(_layer_norm(acc_ref[...], cg_ref[...], cb_ref[...])).astype(BF16)
    y = _dot(zc, wc2_ref[...]) + bc2_ref[...]
    r = alpha * h_ref[...] + mod_ref[0, 5:6, :] * y
    out_ref[...] = _layer_norm(r, lng_ref[...], lnb_ref[...])


def _conv(h, mod, ln_g, ln_b, w_c1, b_c1, w_dw, b_dw, cg, cb, w_c2, b_c2, *, seq, alpha):
    n, d = h.shape
    tm = min(ROW_TILE, seq)
    hb = tm // CONV_HALO
    last = n // CONV_HALO - 1
    row = lambda a: a.reshape(1, -1)
    return pl.pallas_call(
        functools.partial(_conv_kernel, seq=seq, alpha=alpha),
        grid=(n // tm,),
        in_specs=[
            pl.BlockSpec((tm, d), lambda i: (i, 0)),
            pl.BlockSpec((CONV_HALO, d), lambda i: (jnp.maximum(i * hb - 1, 0), 0)),
            pl.BlockSpec((CONV_HALO, d), lambda i: (jnp.minimum((i + 1) * hb, last), 0)),
            _mod_spec(mod, seq // tm), _const_spec((1, d)), _const_spec((1, d)),
            _const_spec(w_c1.shape), _const_spec((1, 2 * d)), _const_spec(w_dw.shape), _const_spec((1, d)),
            _const_spec((1, d)), _const_spec((1, d)), _const_spec(w_c2.shape), _const_spec((1, d)),
        ],
        out_specs=pl.BlockSpec((tm, d), lambda i: (i, 0)),
        out_shape=jax.ShapeDtypeStruct((n, d), F32),
        scratch_shapes=[pltpu.VMEM((tm + 2 * CONV_HALO, d), BF16), pltpu.VMEM((tm + 2 * CONV_HALO, d), F32),
                        pltpu.VMEM((tm, d), F32)],
        compiler_params=_params("parallel"),
        name="conv",
    )(h, h, h, mod, row(ln_g), row(ln_b), w_c1, row(b_c1), w_dw, row(b_dw), row(cg), row(cb), w_c2, row(b_c2))


def kernel(x, c, ctx, c_ctx, w_ada, b_ada, ln_g, ln_b, w_ffn_in, w_ffn_out, w_mix_in, w_mix_out,
           lam_q1, lam_k1, lam_q2, lam_k2, subln_g, w_pool, pool_scale,
           w_c1, b_c1, w_dw, b_dw, conv_ln_g, conv_ln_b, w_c2, b_c2):
    batch, seq, d = x.shape
    ctx_len = ctx.shape[1]
    depth = w_ada.shape[0]
    alpha = (2.0 * depth) ** 0.25
    assert d == D_MODEL and seq % GRID_W == 0
    assert seq % min(ROW_TILE, seq) == 0 and ctx_len % min(ROW_TILE, ctx_len) == 0 and seq % min(ATTN_Q_TILE, seq) == 0
    assert seq % min(FFN_ROW_TILE, seq) == 0 and ctx_len % min(FFN_ROW_TILE, ctx_len) == 0

    n_rows = -(-(batch + 1) // SUBLANES) * SUBLANES
    cc = jnp.zeros((n_rows, d), F32).at[:batch].set(c).at[batch].set(c_ctx)
    mods = _ada(cc, w_ada, b_ada)
    cos, sin = _rope_tables(seq)

    bf = lambda w: w.astype(BF16)
    w_ffn_in, w_ffn_out, w_mix_in, w_mix_out, w_pool = bf(w_ffn_in), bf(w_ffn_out), bf(w_mix_in), bf(w_mix_out), bf(w_pool)
    w_c1, w_c2 = bf(w_c1), bf(w_c2)

    h_lat = x.reshape(batch * seq, d)
    h_ctx = ctx.reshape(batch * ctx_len, d)
    for l in range(depth):
        even = l % 2 == 0
        ctx_later = any(j % 2 == 0 for j in range(l + 1, depth))
        m_lat = mods[l, :batch].reshape(batch, N_MOD, d)
        m_ctx = mods[l, batch].reshape(1, N_MOD, d)
        ffn = lambda h, m, sub, j, rpb: _ffn(h, m, ln_g[l, sub], ln_b[l, sub], w_ffn_in, w_ffn_out,
                                             layer=l, which=j, sub=sub, rows_per_batch=rpb, alpha=alpha)

        h_lat = ffn(h_lat, m_lat, 0, 0, seq)
        if even or ctx_later:
            h_ctx = ffn(h_ctx, m_ctx, 0, 0, ctx_len)

        if even:
            e = l // 2
            lam_init = 0.8 - 0.6 * math.exp(-0.3 * l)
            q, k_lat, vt_lat, u = _mix_in(h_lat, m_lat, w_mix_in[e], cos, sin, seq=seq)
            if ctx_later:
                raise NotImplementedError("context outputs of an even mixer are not needed at this depth")
            k_ctx, vt_ctx = _ctx_kv(h_ctx, m_ctx, w_mix_in[e], seq=ctx_len)
            lat3 = lambda a: a.reshape(batch, seq, DIFF_WIDTH)
            o = _attention(lat3(q), lat3(k_lat), vt_lat, k_ctx.reshape(batch, ctx_len, DIFF_WIDTH), vt_ctx,
                           lam_q1[e], lam_k1[e], lam_q2[e], lam_k2[e], subln_g[e], lam_init=lam_init)
            h_lat = _mix_out(h_lat, m_lat, ln_g[l, 1], ln_b[l, 1], o.reshape(batch * seq, DIFF_WIDTH), u,
                             w_pool[e], pool_scale[e], w_mix_out[e], seq=seq, alpha=alpha)
        else:
            o = l // 2
            if ctx_later:
                raise NotImplementedError("context outputs of an odd mixer are not needed at this depth")
            h_lat = _conv(h_lat, m_lat, ln_g[l, 1], ln_b[l, 1], w_c1[o], b_c1[o], w_dw[o], b_dw[o],
                          conv_ln_g[o], conv_ln_b[o], w_c2[o], b_c2[o], seq=seq, alpha=alpha)

        h_lat = ffn(h_lat, m_lat, 2, 1, seq)
        if ctx_later:
            h_ctx = ffn(h_ctx, m_ctx, 2, 1, ctx_len)
    return h_lat.reshape(batch, seq, d)
```

```python
import functools
import math

import jax
import jax.numpy as jnp
from jax import lax
from jax.experimental import pallas as pl
from jax.experimental.pallas import tpu as pltpu

D_MODEL = 1024
GRID_W = 64
N_DIFF_HEADS = 4
DIFF_HEAD_DIM = 64
DIFF_V_DIM = 2 * DIFF_HEAD_DIM
DIFF_WIDTH = N_DIFF_HEADS * DIFF_V_DIM
ROPE_FREQS = DIFF_HEAD_DIM // 4
ROPE_THETA = 10000.0
POOL_WINDOWS = (2, 4, 8, 16)
POOL_GROUP_DIM = D_MODEL // 8
POOL_WIDTH = len(POOL_WINDOWS) * POOL_GROUP_DIM
CONV_WIDTH = 31
D_FF = 2816
N_MOD = 9
FFN_HALF = 0.5
LN_EPS = 1e-5

F32 = jnp.float32
BF16 = jnp.bfloat16

SUBLANES = 8
LANES = 128
VMEM_LIMIT_BYTES = 56 * 1024 * 1024

ROW_TILE = 512
FFN_ROW_TILE = 1024
FFN_GROUP_ROWS = 256
MIX_OUT_ROW_TILE = 1024
MIX_OUT_GROUP_ROWS = 256
ATTN_Q_TILE = 256
ATTN_Q_TILES_PER_STEP = 2
ATTN_KEY_CHUNK = 512
ADA_COL_TILE = 1152
POOL_HALO = 8
CONV_HALO = 16
CONV_ROW_CHUNK = 256
CONV_CH_BLOCK = 256


def _params(*sem):
    return pltpu.CompilerParams(dimension_semantics=sem, vmem_limit_bytes=VMEM_LIMIT_BYTES)


def _const_spec(shape):
    nd = len(shape)
    return pl.BlockSpec(shape, lambda *_: (0,) * nd, pipeline_mode=pl.Buffered(1))


def _layer_norm(r, g, b):
    mu = jnp.mean(r, axis=-1, keepdims=True)
    d = r - mu
    var = jnp.mean(d * d, axis=-1, keepdims=True)
    return d * lax.rsqrt(var + LN_EPS) * g + b


def _dot(a, b):
    return jnp.dot(a, b, preferred_element_type=F32)


def _dot_nt(a, b):
    return lax.dot_general(a, b, (((1,), (1,)), ((), ())), preferred_element_type=F32)


def _ada_kernel(c_ref, w_ref, b_ref, o_ref):
    s = jax.nn.silu(c_ref[...]).astype(BF16)
    o_ref[0] = _dot(s, w_ref[0].astype(BF16)) + b_ref[0]


def _ada(cc, w_ada, b_ada):
    depth, d, n = w_ada.shape
    r = cc.shape[0]
    return pl.pallas_call(
        _ada_kernel,
        grid=(depth, n // ADA_COL_TILE),
        in_specs=[
            pl.BlockSpec((r, d), lambda l, j: (0, 0)),
            pl.BlockSpec((1, d, ADA_COL_TILE), lambda l, j: (l, 0, j)),
            pl.BlockSpec((1, 1, ADA_COL_TILE), lambda l, j: (l, 0, j)),
        ],
        out_specs=pl.BlockSpec((1, r, ADA_COL_TILE), lambda l, j: (l, 0, j)),
        out_shape=jax.ShapeDtypeStruct((depth, r, n), F32),
        compiler_params=_params("arbitrary", "arbitrary"),
        name="ada",
    )(cc, w_ada, b_ada.reshape(depth, 1, n))


def _mod_spec(mod, tiles_per_batch):
    if mod.shape[0] == 1:
        return pl.BlockSpec((1, N_MOD, D_MODEL), lambda i: (0, 0, 0))
    return pl.BlockSpec((1, N_MOD, D_MODEL), lambda i: (i // tiles_per_batch, 0, 0))


def _modulate(h, mod_ref, i0):
    return h * (1.0 + mod_ref[0, i0 + 1:i0 + 2, :]) + mod_ref[0, i0:i0 + 1, :]


def _ffn_kernel(h_ref, mod_ref, lng_ref, lnb_ref, win_ref, wout_ref, o_ref, *, i0, alpha, parts):
    tm = h_ref.shape[0]
    groups = [slice(p * tm // parts, (p + 1) * tm // parts) for p in range(parts)]
    hs = [h_ref[rows, :] for rows in groups]
    gate_up = []
    for h in hs:
        x = _modulate(h, mod_ref, i0).astype(BF16)
        gate_up.append((_dot(x, win_ref[:, :D_FF]), _dot(x, win_ref[:, D_FF:])))
    ys = [_dot((jax.nn.silu(g) * u).astype(BF16), wout_ref[...]) for g, u in gate_up]
    for rows, h, y in zip(groups, hs, ys):
        r = alpha * h + (FFN_HALF * mod_ref[0, i0 + 2:i0 + 3, :]) * y
        o_ref[rows, :] = _layer_norm(r, lng_ref[...], lnb_ref[...])


def _ffn(h, mod, ln_g, ln_b, w_in, w_out, *, layer, which, sub, rows_per_batch, alpha):
    n, d = h.shape
    tm = min(FFN_ROW_TILE, rows_per_batch)
    pick = lambda w: pl.BlockSpec((None, None) + w.shape[2:], lambda i: (layer, which, 0, 0),
                                  pipeline_mode=pl.Buffered(1))
    return pl.pallas_call(
        functools.partial(_ffn_kernel, i0=3 * sub, alpha=alpha, parts=max(1, tm // FFN_GROUP_ROWS)),
        grid=(n // tm,),
        in_specs=[
            pl.BlockSpec((tm, d), lambda i: (i, 0)),
            _mod_spec(mod, rows_per_batch // tm),
            _const_spec((1, d)), _const_spec((1, d)),
            pick(w_in), pick(w_out),
        ],
        out_specs=pl.BlockSpec((tm, d), lambda i: (i, 0)),
        out_shape=jax.ShapeDtypeStruct((n, d), F32),
        compiler_params=_params("parallel"),
        name="ffn",
    )(h, mod, ln_g.reshape(1, d), ln_b.reshape(1, d), w_in, w_out)


def _rope_tables(seq):
    t = jnp.arange(seq, dtype=F32)
    row = jnp.floor(t / GRID_W)
    col = t - row * GRID_W
    inv_freq = ROPE_THETA ** (-jnp.arange(ROPE_FREQS, dtype=F32) / ROPE_FREQS)
    lane = jnp.arange(LANES)
    seg = lane % DIFF_HEAD_DIM
    pos = jnp.where((seg // (2 * ROPE_FREQS))[None, :] == 0, row[:, None], col[:, None])
    ang = pos * inv_freq[seg % ROPE_FREQS][None, :]
    sign = jnp.where((seg % (2 * ROPE_FREQS)) < ROPE_FREQS, -1.0, 1.0).astype(F32)
    return jnp.cos(ang), jnp.sin(ang) * sign[None, :]


def _mix_in_kernel(h_ref, mod_ref, w_ref, cos_ref, sin_ref, q_ref, k_ref, vt_ref, u_ref):
    x = _modulate(h_ref[...], mod_ref, 3).astype(BF16)
    z = _dot(x, w_ref[...])
    cos = cos_ref[...]
    sin = sin_ref[...]
    first_half = (lax.broadcasted_iota(jnp.int32, cos.shape, 1) % (2 * ROPE_FREQS)) < ROPE_FREQS

    def rope(blk):
        swapped = jnp.where(first_half, pltpu.roll(blk, LANES - ROPE_FREQS, 1), pltpu.roll(blk, ROPE_FREQS, 1))
        return blk * cos + swapped * sin

    q_scale = DIFF_HEAD_DIM ** -0.5 * math.log2(math.e)
    for j in range(DIFF_WIDTH // LANES):
        sl = slice(j * LANES, (j + 1) * LANES)
        q_ref[:, sl] = (rope(z[:, sl]) * q_scale).astype(BF16)
        k_ref[:, sl] = rope(z[:, DIFF_WIDTH + j * LANES:DIFF_WIDTH + (j + 1) * LANES]).astype(BF16)
    vt_ref[0] = z[:, 2 * DIFF_WIDTH:3 * DIFF_WIDTH].T.astype(BF16)
    u_ref[...] = z[:, 3 * DIFF_WIDTH:]


def _mix_in(h, mod, w, cos, sin, *, seq):
    n, d = h.shape
    tm = min(ROW_TILE, seq)
    tpb = seq // tm
    row_spec = lambda width: pl.BlockSpec((tm, width), lambda i: (i, 0))
    tab_spec = pl.BlockSpec((tm, LANES), lambda i: (i % tpb, 0))
    vt_spec = pl.BlockSpec((1, DIFF_WIDTH, tm), lambda i: (i // tpb, 0, i % tpb))
    return pl.pallas_call(
        _mix_in_kernel,
        grid=(n // tm,),
        in_specs=[row_spec(d), _mod_spec(mod, tpb), _const_spec(w.shape), tab_spec, tab_spec],
        out_specs=[row_spec(DIFF_WIDTH), row_spec(DIFF_WIDTH), vt_spec, row_spec(POOL_WIDTH)],
        out_shape=[jax.ShapeDtypeStruct((n, DIFF_WIDTH), BF16), jax.ShapeDtypeStruct((n, DIFF_WIDTH), BF16),
                   jax.ShapeDtypeStruct((n // seq, DIFF_WIDTH, seq), BF16), jax.ShapeDtypeStruct((n, POOL_WIDTH), F32)],
        compiler_params=_params("parallel"),
        name="mix_in",
    )(h, mod, w, cos, sin)


def _ctx_kv_kernel(h_ref, mod_ref, w_ref, k_ref, vt_ref):
    x = _modulate(h_ref[...], mod_ref, 3).astype(BF16)
    z = _dot(x, w_ref[:, DIFF_WIDTH:3 * DIFF_WIDTH])
    k_ref[...] = z[:, :DIFF_WIDTH].astype(BF16)
    vt_ref[0] = z[:, DIFF_WIDTH:].T.astype(BF16)


def _ctx_kv(h, mod, w, *, seq):
    n, d = h.shape
    tm = min(ROW_TILE, seq)
    tpb = seq // tm
    return pl.pallas_call(
        _ctx_kv_kernel,
        grid=(n // tm,),
        in_specs=[pl.BlockSpec((tm, d), lambda i: (i, 0)), _mod_spec(mod, tpb), _const_spec(w.shape)],
        out_specs=[pl.BlockSpec((tm, DIFF_WIDTH), lambda i: (i, 0)),
                   pl.BlockSpec((1, DIFF_WIDTH, tm), lambda i: (i // tpb, 0, i % tpb))],
        out_shape=[jax.ShapeDtypeStruct((n, DIFF_WIDTH), BF16), jax.ShapeDtypeStruct((n // seq, DIFF_WIDTH, seq), BF16)],
        compiler_params=_params("parallel"),
        name="ctx_kv",
    )(h, mod, w)


def _attn_kernel(lq1_ref, lk1_ref, lq2_ref, lk2_ref, g_ref, q_ref, kl_ref, vtl_ref, kc_ref, vtc_ref, o_ref,
                 s_scr, e_scr, *, lam_init):
    t, s_ctx = kl_ref.shape[1], kc_ref.shape[1]
    lam = (jnp.exp(jnp.sum(lq1_ref[...] * lk1_ref[...], axis=-1, keepdims=True))
           - jnp.exp(jnp.sum(lq2_ref[...] * lk2_ref[...], axis=-1, keepdims=True)) + lam_init)
    lane = lax.broadcasted_iota(jnp.int32, (1, DIFF_V_DIM), 1)
    out_gain = g_ref[...] * (1.0 - lam_init)
    tq = s_scr.shape[2]
    units = [(qt, hd, mp) for qt in range(q_ref.shape[1] // tq) for hd in range(N_DIFF_HEADS) for mp in range(2)]
    chunks = [(kc_ref, vtc_ref, 0, 0, s_ctx)]
    chunks += [(kl_ref, vtl_ref, c * ATTN_KEY_CHUNK, s_ctx + c * ATTN_KEY_CHUNK, ATTN_KEY_CHUNK)
               for c in range(t // ATTN_KEY_CHUNK)]
    col_max, col_sum, acc, norm = {}, {}, {}, {}

    def scores(ui, ci):
        qt, hd, mp = units[ui]
        cols = slice(hd * DIFF_V_DIM, (hd + 1) * DIFF_V_DIM)
        k_ref, _, off, soff, n = chunks[ci]
        q = q_ref[0, qt * tq:(qt + 1) * tq, cols]
        q = jnp.where((lane < DIFF_HEAD_DIM) if mp == 0 else (lane >= DIFF_HEAD_DIM), q, 0)
        s = _dot_nt(k_ref[0, off:off + n, cols], q)
        s_scr[ui % 2, soff:soff + n, :] = s
        m = jnp.max(s, axis=0, keepdims=True)
        col_max[ui] = m if ci == 0 else jnp.maximum(col_max[ui], m)

    def exponentials(ui, ci):
        _, _, _, soff, n = chunks[ci]
        e = jnp.exp2(s_scr[ui % 2, soff:soff + n, :] - col_max[ui])
        e_scr[ui % 4, soff:soff + n, :] = e.astype(BF16)
        l = jnp.sum(e, axis=0, keepdims=True)
        col_sum[ui] = l if ci == 0 else col_sum[ui] + l

    def values(pi, ci):
        qt, hd, _ = units[2 * pi]
        rows = slice(hd * DIFF_V_DIM, (hd + 1) * DIFF_V_DIM)
        _, vt_ref, off, soff, n = chunks[ci]
        if ci == 0:
            l1, l2 = col_sum.pop(2 * pi), col_sum.pop(2 * pi + 1)
            norm[pi] = ((lam * l1 / l2).astype(BF16), l1)
        p = e_scr[(2 * pi) % 4, soff:soff + n, :] - e_scr[(2 * pi + 1) % 4, soff:soff + n, :] * norm[pi][0]
        d = _dot(vt_ref[0, rows, off:off + n], p)
        acc[pi] = d if ci == 0 else acc[pi] + d
        if ci == len(chunks) - 1:
            o = acc.pop(pi) / norm.pop(pi)[1]
            o = o * lax.rsqrt(jnp.mean(o * o, axis=0, keepdims=True) + LN_EPS) * out_gain
            o_ref[0, qt * tq:(qt + 1) * tq, rows] = o.T.astype(BF16)

    for r in range(len(units) + 2):
        for ci in range(len(chunks)):
            if r < len(units):
                scores(r, ci)
            if 1 <= r <= len(units):
                exponentials(r - 1, ci)
            if r >= 3 and r % 2 == 1:
                values((r - 3) // 2, ci)


def _attention(q, k_lat, vt_lat, k_ctx, vt_ctx, lq1, lk1, lq2, lk2, subln_g, *, lam_init):
    b, t, w = q.shape
    s_ctx = k_ctx.shape[1]
    tq = min(ATTN_Q_TILE, t)
    tstep = min(ATTN_Q_TILES_PER_STEP * tq, t)
    assert t % ATTN_KEY_CHUNK == 0 and t % tstep == 0
    vec = lambda a: a.reshape(1, -1).astype(F32)
    lam_spec = pl.BlockSpec((1, DIFF_HEAD_DIM), lambda bi, qi: (0, 0))
    return pl.pallas_call(
        functools.partial(_attn_kernel, lam_init=lam_init),
        grid=(b, t // tstep),
        in_specs=[
            lam_spec, lam_spec, lam_spec, lam_spec,
            pl.BlockSpec((DIFF_V_DIM, 1), lambda bi, qi: (0, 0)),
            pl.BlockSpec((1, tstep, w), lambda bi, qi: (bi, qi, 0)),
            pl.BlockSpec((1, t, w), lambda bi, qi: (bi, 0, 0)),
            pl.BlockSpec((1, w, t), lambda bi, qi: (bi, 0, 0)),
            pl.BlockSpec((1, s_ctx, w), lambda bi, qi: (bi, 0, 0)),
            pl.BlockSpec((1, w, s_ctx), lambda bi, qi: (bi, 0, 0)),
        ],
        out_specs=pl.BlockSpec((1, tstep, w), lambda bi, qi: (bi, qi, 0)),
        out_shape=jax.ShapeDtypeStruct((b, t, w), BF16),
        scratch_shapes=[pltpu.VMEM((2, t + s_ctx, tq), F32), pltpu.VMEM((4, t + s_ctx, tq), BF16)],
        compiler_params=_params("parallel", "arbitrary"),
        name="diff_attn",
    )(vec(lq1), vec(lk1), vec(lq2), vec(lk2), subln_g.reshape(DIFF_V_DIM, 1).astype(F32),
      q, k_lat, vt_lat, k_ctx, vt_ctx)


def _mix_out_kernel(h_ref, mod_ref, lng_ref, lnb_ref, o_ref, u_ref, up_ref, un_ref, wpool_ref, pscale_ref, wout_ref,
                    out_ref, ext_ref, *, seq, alpha):
    tm = h_ref.shape[0]
    tpb = seq // tm
    ib = pl.program_id(0) % tpb
    ext_ref[0:POOL_HALO, :] = jnp.where(ib > 0, up_ref[...], 0.0)
    ext_ref[POOL_HALO:POOL_HALO + tm, :] = u_ref[...]
    ext_ref[POOL_HALO + tm:, :] = jnp.where(ib < tpb - 1, un_ref[...], 0.0)

    n = min(MIX_OUT_GROUP_ROWS, tm)
    ys = []
    for r0 in range(0, tm, n):
        pos = ib * tm + r0 + lax.broadcasted_iota(jnp.int32, (n, 1), 0)
        y = _dot(o_ref[r0:r0 + n, :], wout_ref[:DIFF_WIDTH, :])
        for g, win in enumerate(POOL_WINDOWS):
            sl = slice(g * POOL_GROUP_DIM, (g + 1) * POOL_GROUP_DIM)
            lo = POOL_HALO + r0 - win // 2
            acc = ext_ref[lo:lo + n, sl]
            for d in range(1, win):
                acc = acc + ext_ref[lo + d:lo + d + n, sl]
            cnt = jnp.minimum(pos + win // 2, seq) - jnp.maximum(pos - win // 2, 0)
            pooled = acc / cnt.astype(F32) - u_ref[r0:r0 + n, sl]
            yg = _dot(pooled.astype(BF16), wpool_ref[g]) * pscale_ref[:, sl]
            y = y + _dot(yg.astype(BF16),
                         wout_ref[DIFF_WIDTH + g * POOL_GROUP_DIM:DIFF_WIDTH + (g + 1) * POOL_GROUP_DIM, :])
        ys.append(y)
    for i, y in enumerate(ys):
        rows = slice(i * n, (i + 1) * n)
        r = alpha * h_ref[rows, :] + mod_ref[0, 5:6, :] * y
        out_ref[rows, :] = _layer_norm(r, lng_ref[...], lnb_ref[...])


def _mix_out(h, mod, ln_g, ln_b, o, u, w_pool, pool_scale, w_out, *, seq, alpha):
    n, d = h.shape
    tm = min(MIX_OUT_ROW_TILE, seq)
    hb = tm // POOL_HALO
    last = n // POOL_HALO - 1
    row_spec = lambda width: pl.BlockSpec((tm, width), lambda i: (i, 0))
    return pl.pallas_call(
        functools.partial(_mix_out_kernel, seq=seq, alpha=alpha),
        grid=(n // tm,),
        in_specs=[
            row_spec(d), _mod_spec(mod, seq // tm), _const_spec((1, d)), _const_spec((1, d)),
            row_spec(DIFF_WIDTH), row_spec(POOL_WIDTH),
            pl.BlockSpec((POOL_HALO, POOL_WIDTH), lambda i: (jnp.maximum(i * hb - 1, 0), 0)),
            pl.BlockSpec((POOL_HALO, POOL_WIDTH), lambda i: (jnp.minimum((i + 1) * hb, last), 0)),
            _const_spec(w_pool.shape), _const_spec((1, POOL_WIDTH)), _const_spec(w_out.shape),
        ],
        out_specs=row_spec(d),
        out_shape=jax.ShapeDtypeStruct((n, d), F32),
        scratch_shapes=[pltpu.VMEM((tm + 2 * POOL_HALO, POOL_WIDTH), F32)],
        compiler_params=_params("parallel"),
        name="mix_out",
    )(h, mod, ln_g.reshape(1, d), ln_b.reshape(1, d), o, u, u, u, w_pool, pool_scale.reshape(1, -1), w_out)


def _conv_kernel(h_ref, hp_ref, hn_ref, mod_ref, lng_ref, lnb_ref, wc1_ref, bc1_ref, wdw_ref, bdw_ref,
                 cg_ref, cb_ref, wc2_ref, bc2_ref, out_ref, x_ref, z_ref, acc_ref, *, seq, alpha):
    tm, d = h_ref.shape
    tpb = seq // tm
    ib = pl.program_id(0) % tpb
    rows = tm + 2 * CONV_HALO

    x_ref[0:CONV_HALO, :] = _modulate(hp_ref[...], mod_ref, 3).astype(BF16)
    x_ref[CONV_HALO:CONV_HALO + tm, :] = _modulate(h_ref[...], mod_ref, 3).astype(BF16)
    x_ref[CONV_HALO + tm:, :] = _modulate(hn_ref[...], mod_ref, 3).astype(BF16)
    ridx = lax.broadcasted_iota(jnp.int32, (rows, 1), 0)
    inside = jnp.logical_and(jnp.logical_or(ridx >= CONV_HALO, ib > 0),
                             jnp.logical_or(ridx < CONV_HALO + tm, ib < tpb - 1))

    first = CONV_HALO - CONV_WIDTH // 2
    span = CONV_ROW_CHUNK + SUBLANES

    def glu_block(j):
        ca = slice(j * CONV_CH_BLOCK, (j + 1) * CONV_CH_BLOCK)
        cg = slice(d + j * CONV_CH_BLOCK, d + (j + 1) * CONV_CH_BLOCK)
        x = x_ref[...]
        a = _dot(x, wc1_ref[:, ca]) + bc1_ref[:, ca]
        g = _dot(x, wc1_ref[:, cg]) + bc1_ref[:, cg]
        z_ref[:, ca] = jnp.where(inside, a * jax.nn.sigmoid(g), 0.0)

    def conv_block(j):
        for c in range(j * CONV_CH_BLOCK // LANES, (j + 1) * CONV_CH_BLOCK // LANES):
            cols = slice(c * LANES, (c + 1) * LANES)
            for r0 in range(0, tm, CONV_ROW_CHUNK):
                acc = jnp.broadcast_to(bdw_ref[:, cols], (CONV_ROW_CHUNK, LANES))
                for b in range(SUBLANES):
                    part = None
                    for a in range(-(-(first + CONV_WIDTH) // SUBLANES)):
                        k = SUBLANES * a + b - first
                        if 0 <= k < CONV_WIDTH:
                            term = z_ref[r0 + SUBLANES * a:r0 + SUBLANES * a + span, cols] * wdw_ref[k:k + 1, cols]
                            part = term if part is None else part + term
                    acc = acc + part[b:b + CONV_ROW_CHUNK]
                acc_ref[r0:r0 + CONV_ROW_CHUNK, cols] = acc

    n_blocks = d // CONV_CH_BLOCK
    glu_block(0)
    for j in range(n_blocks):
        if j + 1 < n_blocks:
            glu_block(j + 1)
        conv_block(j)

    zc = jax.nn.silu(_layer_norm(acc_ref[...], cg_ref[...], cb_ref[...])).astype(BF16)
    y = _dot(zc, wc2_ref[...]) + bc2_ref[...]
    r = alpha * h_ref[...] + mod_ref[0, 5:6, :] * y
    out_ref[...] = _layer_norm(r, lng_ref[...], lnb_ref[...])


def _conv(h, mod, ln_g, ln_b, w_c1, b_c1, w_dw, b_dw, cg, cb, w_c2, b_c2, *, seq, alpha):
    n, d = h.shape
    tm = min(ROW_TILE, seq)
    hb = tm // CONV_HALO
    last = n // CONV_HALO - 1
    row = lambda a: a.reshape(1, -1)
    return pl.pallas_call(
        functools.partial(_conv_kernel, seq=seq, alpha=alpha),
        grid=(n // tm,),
        in_specs=[
            pl.BlockSpec((tm, d), lambda i: (i, 0)),
            pl.BlockSpec((CONV_HALO, d), lambda i: (jnp.maximum(i * hb - 1, 0), 0)),
            pl.BlockSpec((CONV_HALO, d), lambda i: (jnp.minimum((i + 1) * hb, last), 0)),
            _mod_spec(mod, seq // tm), _const_spec((1, d)), _const_spec((1, d)),
            _const_spec(w_c1.shape), _const_spec((1, 2 * d)), _const_spec(w_dw.shape), _const_spec((1, d)),
            _const_spec((1, d)), _const_spec((1, d)), _const_spec(w_c2.shape), _const_spec((1, d)),
        ],
        out_specs=pl.BlockSpec((tm, d), lambda i: (i, 0)),
        out_shape=jax.ShapeDtypeStruct((n, d), F32),
        scratch_shapes=[pltpu.VMEM((tm + 2 * CONV_HALO, d), BF16), pltpu.VMEM((tm + 2 * CONV_HALO, d), F32),
                        pltpu.VMEM((tm, d), F32)],
        compiler_params=_params("parallel"),
        name="conv",
    )(h, h, h, mod, row(ln_g), row(ln_b), w_c1, row(b_c1), w_dw, row(b_dw), row(cg), row(cb), w_c2, row(b_c2))


def kernel(x, c, ctx, c_ctx, w_ada, b_ada, ln_g, ln_b, w_ffn_in, w_ffn_out, w_mix_in, w_mix_out,
           lam_q1, lam_k1, lam_q2, lam_k2, subln_g, w_pool, pool_scale,
           w_c1, b_c1, w_dw, b_dw, conv_ln_g, conv_ln_b, w_c2, b_c2):
    batch, seq, d = x.shape
    ctx_len = ctx.shape[1]
    depth = w_ada.shape[0]
    alpha = (2.0 * depth) ** 0.25
    assert d == D_MODEL and seq % GRID_W == 0
    assert seq % min(ROW_TILE, seq) == 0 and ctx_len % min(ROW_TILE, ctx_len) == 0 and seq % min(ATTN_Q_TILE, seq) == 0
    assert seq % min(FFN_ROW_TILE, seq) == 0 and ctx_len % min(FFN_ROW_TILE, ctx_len) == 0
    assert seq % min(MIX_OUT_ROW_TILE, seq) == 0

    n_rows = -(-(batch + 1) // SUBLANES) * SUBLANES
    cc = jnp.zeros((n_rows, d), F32).at[:batch].set(c).at[batch].set(c_ctx)
    mods = _ada(cc, w_ada, b_ada)
    cos, sin = _rope_tables(seq)

    bf = lambda w: w.astype(BF16)
    w_ffn_in, w_ffn_out, w_mix_in, w_mix_out, w_pool = bf(w_ffn_in), bf(w_ffn_out), bf(w_mix_in), bf(w_mix_out), bf(w_pool)
    w_c1, w_c2 = bf(w_c1), bf(w_c2)

    h_lat = x.reshape(batch * seq, d)
    h_ctx = ctx.reshape(batch * ctx_len, d)
    for l in range(depth):
        even = l % 2 == 0
        ctx_later = any(j % 2 == 0 for j in range(l + 1, depth))
        m_lat = mods[l, :batch].reshape(batch, N_MOD, d)
        m_ctx = mods[l, batch].reshape(1, N_MOD, d)
        ffn = lambda h, m, sub, j, rpb: _ffn(h, m, ln_g[l, sub], ln_b[l, sub], w_ffn_in, w_ffn_out,
                                             layer=l, which=j, sub=sub, rows_per_batch=rpb, alpha=alpha)

        h_lat = ffn(h_lat, m_lat, 0, 0, seq)
        if even or ctx_later:
            h_ctx = ffn(h_ctx, m_ctx, 0, 0, ctx_len)

        if even:
            e = l // 2
            lam_init = 0.8 - 0.6 * math.exp(-0.3 * l)
            q, k_lat, vt_lat, u = _mix_in(h_lat, m_lat, w_mix_in[e], cos, sin, seq=seq)
            if ctx_later:
                raise NotImplementedError("context outputs of an even mixer are not needed at this depth")
            k_ctx, vt_ctx = _ctx_kv(h_ctx, m_ctx, w_mix_in[e], seq=ctx_len)
            lat3 = lambda a: a.reshape(batch, seq, DIFF_WIDTH)
            o = _attention(lat3(q), lat3(k_lat), vt_lat, k_ctx.reshape(batch, ctx_len, DIFF_WIDTH), vt_ctx,
                           lam_q1[e], lam_k1[e], lam_q2[e], lam_k2[e], subln_g[e], lam_init=lam_init)
            h_lat = _mix_out(h_lat, m_lat, ln_g[l, 1], ln_b[l, 1], o.reshape(batch * seq, DIFF_WIDTH), u,
                             w_pool[e], pool_scale[e], w_mix_out[e], seq=seq, alpha=alpha)
        else:
            o = l // 2
            if ctx_later:
                raise NotImplementedError("context outputs of an odd mixer are not needed at this depth")
            h_lat = _conv(h_lat, m_lat, ln_g[l, 1], ln_b[l, 1], w_c1[o], b_c1[o], w_dw[o], b_dw[o],
                          conv_ln_g[o], conv_ln_b[o], w_c2[o], b_c2[o], seq=seq, alpha=alpha)

        h_lat = ffn(h_lat, m_lat, 2, 1, seq)
        if ctx_later:
            h_ctx = ffn(h_ctx, m_ctx, 2, 1, ctx_len)
    return h_lat.reshape(batch, seq, d)
```

```python
import functools
import math

import jax
import jax.numpy as jnp
from jax import lax
from jax.experimental import pallas as pl
from jax.experimental.pallas import tpu as pltpu

D_MODEL = 1024
GRID_W = 64
N_DIFF_HEADS = 4
DIFF_HEAD_DIM = 64
DIFF_V_DIM = 2 * DIFF_HEAD_DIM
DIFF_WIDTH = N_DIFF_HEADS * DIFF_V_DIM
ROPE_FREQS = DIFF_HEAD_DIM // 4
ROPE_THETA = 10000.0
POOL_WINDOWS = (2, 4, 8, 16)
POOL_GROUP_DIM = D_MODEL // 8
POOL_WIDTH = len(POOL_WINDOWS) * POOL_GROUP_DIM
CONV_WIDTH = 31
D_FF = 2816
N_MOD = 9
FFN_HALF = 0.5
LN_EPS = 1e-5

F32 = jnp.float32
BF16 = jnp.bfloat16

SUBLANES = 8
LANES = 128
VMEM_LIMIT_BYTES = 56 * 1024 * 1024

ROW_TILE = 512
FFN_ROW_TILE = 1024
FFN_GROUP_ROWS = 256
ATTN_Q_TILE = 256
ATTN_Q_TILES_PER_STEP = 2
ATTN_KEY_CHUNK = 1024
ADA_COL_TILE = 1152
POOL_HALO = 8
CONV_HALO = 16
CONV_ROW_CHUNK = 256
CONV_CH_BLOCK = 256


def _params(*sem):
    return pltpu.CompilerParams(dimension_semantics=sem, vmem_limit_bytes=VMEM_LIMIT_BYTES)


def _const_spec(shape):
    nd = len(shape)
    return pl.BlockSpec(shape, lambda *_: (0,) * nd, pipeline_mode=pl.Buffered(1))


def _layer_norm(r, g, b):
    mu = jnp.mean(r, axis=-1, keepdims=True)
    d = r - mu
    var = jnp.mean(d * d, axis=-1, keepdims=True)
    return d * lax.rsqrt(var + LN_EPS) * g + b


def _dot(a, b):
    return jnp.dot(a, b, preferred_element_type=F32)


def _dot_nt(a, b):
    return lax.dot_general(a, b, (((1,), (1,)), ((), ())), preferred_element_type=F32)


def _ada_kernel(c_ref, w_ref, b_ref, o_ref):
    s = jax.nn.silu(c_ref[...]).astype(BF16)
    o_ref[0] = _dot(s, w_ref[0].astype(BF16)) + b_ref[0]


def _ada(cc, w_ada, b_ada):
    depth, d, n = w_ada.shape
    r = cc.shape[0]
    return pl.pallas_call(
        _ada_kernel,
        grid=(depth, n // ADA_COL_TILE),
        in_specs=[
            pl.BlockSpec((r, d), lambda l, j: (0, 0)),
            pl.BlockSpec((1, d, ADA_COL_TILE), lambda l, j: (l, 0, j)),
            pl.BlockSpec((1, 1, ADA_COL_TILE), lambda l, j: (l, 0, j)),
        ],
        out_specs=pl.BlockSpec((1, r, ADA_COL_TILE), lambda l, j: (l, 0, j)),
        out_shape=jax.ShapeDtypeStruct((depth, r, n), F32),
        compiler_params=_params("arbitrary", "arbitrary"),
        name="ada",
    )(cc, w_ada, b_ada.reshape(depth, 1, n))


def _mod_spec(mod, tiles_per_batch):
    if mod.shape[0] == 1:
        return pl.BlockSpec((1, N_MOD, D_MODEL), lambda i: (0, 0, 0))
    return pl.BlockSpec((1, N_MOD, D_MODEL), lambda i: (i // tiles_per_batch, 0, 0))


def _modulate(h, mod_ref, i0):
    return h * (1.0 + mod_ref[0, i0 + 1:i0 + 2, :]) + mod_ref[0, i0:i0 + 1, :]


def _ffn_kernel(h_ref, mod_ref, lng_ref, lnb_ref, win_ref, wout_ref, o_ref, *, i0, alpha, parts):
    tm = h_ref.shape[0]
    groups = [slice(p * tm // parts, (p + 1) * tm // parts) for p in range(parts)]
    hs = [h_ref[rows, :] for rows in groups]
    gate_up = []
    for h in hs:
        x = _modulate(h, mod_ref, i0).astype(BF16)
        gate_up.append((_dot(x, win_ref[:, :D_FF]), _dot(x, win_ref[:, D_FF:])))
    ys = [_dot((jax.nn.silu(g) * u).astype(BF16), wout_ref[...]) for g, u in gate_up]
    for rows, h, y in zip(groups, hs, ys):
        r = alpha * h + (FFN_HALF * mod_ref[0, i0 + 2:i0 + 3, :]) * y
        o_ref[rows, :] = _layer_norm(r, lng_ref[...], lnb_ref[...])


def _ffn(h, mod, ln_g, ln_b, w_in, w_out, *, layer, which, sub, rows_per_batch, alpha):
    n, d = h.shape
    tm = min(FFN_ROW_TILE, rows_per_batch)
    pick = lambda w: pl.BlockSpec((None, None) + w.shape[2:], lambda i: (layer, which, 0, 0),
                                  pipeline_mode=pl.Buffered(1))
    return pl.pallas_call(
        functools.partial(_ffn_kernel, i0=3 * sub, alpha=alpha, parts=max(1, tm // FFN_GROUP_ROWS)),
        grid=(n // tm,),
        in_specs=[
            pl.BlockSpec((tm, d), lambda i: (i, 0)),
            _mod_spec(mod, rows_per_batch // tm),
            _const_spec((1, d)), _const_spec((1, d)),
            pick(w_in), pick(w_out),
        ],
        out_specs=pl.BlockSpec((tm, d), lambda i: (i, 0)),
        out_shape=jax.ShapeDtypeStruct((n, d), F32),
        compiler_params=_params("parallel"),
        name="ffn",
    )(h, mod, ln_g.reshape(1, d), ln_b.reshape(1, d), w_in, w_out)


def _rope_tables(seq):
    t = jnp.arange(seq, dtype=F32)
    row = jnp.floor(t / GRID_W)
    col = t - row * GRID_W
    inv_freq = ROPE_THETA ** (-jnp.arange(ROPE_FREQS, dtype=F32) / ROPE_FREQS)
    lane = jnp.arange(LANES)
    seg = lane % DIFF_HEAD_DIM
    pos = jnp.where((seg // (2 * ROPE_FREQS))[None, :] == 0, row[:, None], col[:, None])
    ang = pos * inv_freq[seg % ROPE_FREQS][None, :]
    sign = jnp.where((seg % (2 * ROPE_FREQS)) < ROPE_FREQS, -1.0, 1.0).astype(F32)
    return jnp.cos(ang), jnp.sin(ang) * sign[None, :]


def _mix_in_kernel(h_ref, mod_ref, w_ref, cos_ref, sin_ref, q_ref, k_ref, vt_ref, u_ref):
    x = _modulate(h_ref[...], mod_ref, 3).astype(BF16)
    z = _dot(x, w_ref[...])
    cos = cos_ref[...]
    sin = sin_ref[...]
    first_half = (lax.broadcasted_iota(jnp.int32, cos.shape, 1) % (2 * ROPE_FREQS)) < ROPE_FREQS

    def rope(blk):
        swapped = jnp.where(first_half, pltpu.roll(blk, LANES - ROPE_FREQS, 1), pltpu.roll(blk, ROPE_FREQS, 1))
        return blk * cos + swapped * sin

    q_scale = DIFF_HEAD_DIM ** -0.5 * math.log2(math.e)
    for j in range(DIFF_WIDTH // LANES):
        sl = slice(j * LANES, (j + 1) * LANES)
        q_ref[:, sl] = (rope(z[:, sl]) * q_scale).astype(BF16)
        k_ref[:, sl] = rope(z[:, DIFF_WIDTH + j * LANES:DIFF_WIDTH + (j + 1) * LANES]).astype(BF16)
    vt_ref[0] = z[:, 2 * DIFF_WIDTH:3 * DIFF_WIDTH].T.astype(BF16)
    u_ref[...] = z[:, 3 * DIFF_WIDTH:]


def _mix_in(h, mod, w, cos, sin, *, seq):
    n, d = h.shape
    tm = min(ROW_TILE, seq)
    tpb = seq // tm
    row_spec = lambda width: pl.BlockSpec((tm, width), lambda i: (i, 0))
    tab_spec = pl.BlockSpec((tm, LANES), lambda i: (i % tpb, 0))
    vt_spec = pl.BlockSpec((1, DIFF_WIDTH, tm), lambda i: (i // tpb, 0, i % tpb))
    return pl.pallas_call(
        _mix_in_kernel,
        grid=(n // tm,),
        in_specs=[row_spec(d), _mod_spec(mod, tpb), _const_spec(w.shape), tab_spec, tab_spec],
        out_specs=[row_spec(DIFF_WIDTH), row_spec(DIFF_WIDTH), vt_spec, row_spec(POOL_WIDTH)],
        out_shape=[jax.ShapeDtypeStruct((n, DIFF_WIDTH), BF16), jax.ShapeDtypeStruct((n, DIFF_WIDTH), BF16),
                   jax.ShapeDtypeStruct((n // seq, DIFF_WIDTH, seq), BF16), jax.ShapeDtypeStruct((n, POOL_WIDTH), F32)],
        compiler_params=_params("parallel"),
        name="mix_in",
    )(h, mod, w, cos, sin)


def _ctx_kv_kernel(h_ref, mod_ref, w_ref, k_ref, vt_ref):
    x = _modulate(h_ref[...], mod_ref, 3).astype(BF16)
    z = _dot(x, w_ref[:, DIFF_WIDTH:3 * DIFF_WIDTH])
    k_ref[...] = z[:, :DIFF_WIDTH].astype(BF16)
    vt_ref[0] = z[:, DIFF_WIDTH:].T.astype(BF16)


def _ctx_kv(h, mod, w, *, seq):
    n, d = h.shape
    tm = min(ROW_TILE, seq)
    tpb = seq // tm
    return pl.pallas_call(
        _ctx_kv_kernel,
        grid=(n // tm,),
        in_specs=[pl.BlockSpec((tm, d), lambda i: (i, 0)), _mod_spec(mod, tpb), _const_spec(w.shape)],
        out_specs=[pl.BlockSpec((tm, DIFF_WIDTH), lambda i: (i, 0)),
                   pl.BlockSpec((1, DIFF_WIDTH, tm), lambda i: (i // tpb, 0, i % tpb))],
        out_shape=[jax.ShapeDtypeStruct((n, DIFF_WIDTH), BF16), jax.ShapeDtypeStruct((n // seq, DIFF_WIDTH, seq), BF16)],
        compiler_params=_params("parallel"),
        name="ctx_kv",
    )(h, mod, w)


def _attn_kernel(lq1_ref, lk1_ref, lq2_ref, lk2_ref, g_ref, q_ref, kl_ref, vtl_ref, kc_ref, vtc_ref, o_ref,
                 s_scr, e_scr, *, lam_init):
    t, s_ctx = kl_ref.shape[1], kc_ref.shape[1]
    lam = (jnp.exp(jnp.sum(lq1_ref[...] * lk1_ref[...], axis=-1, keepdims=True))
           - jnp.exp(jnp.sum(lq2_ref[...] * lk2_ref[...], axis=-1, keepdims=True)) + lam_init)
    lane = lax.broadcasted_iota(jnp.int32, (1, DIFF_V_DIM), 1)
    out_gain = g_ref[...] * (1.0 - lam_init)
    tq = s_scr.shape[2]
    units = [(qt, hd, mp) for qt in range(q_ref.shape[1] // tq) for hd in range(N_DIFF_HEADS) for mp in range(2)]
    chunks = [(kc_ref, vtc_ref, 0, 0, s_ctx)]
    chunks += [(kl_ref, vtl_ref, c * ATTN_KEY_CHUNK, s_ctx + c * ATTN_KEY_CHUNK, ATTN_KEY_CHUNK)
               for c in range(t // ATTN_KEY_CHUNK)]
    col_max, col_sum, acc, norm = {}, {}, {}, {}

    def scores(ui, ci):
        qt, hd, mp = units[ui]
        cols = slice(hd * DIFF_V_DIM, (hd + 1) * DIFF_V_DIM)
        k_ref, _, off, soff, n = chunks[ci]
        q = q_ref[0, qt * tq:(qt + 1) * tq, cols]
        q = jnp.where((lane < DIFF_HEAD_DIM) if mp == 0 else (lane >= DIFF_HEAD_DIM), q, 0)
        s = _dot_nt(k_ref[0, off:off + n, cols], q)
        s_scr[ui % 2, soff:soff + n, :] = s
        m = jnp.max(s, axis=0, keepdims=True)
        col_max[ui] = m if ci == 0 else jnp.maximum(col_max[ui], m)

    def exponentials(ui, ci):
        _, _, _, soff, n = chunks[ci]
        e = jnp.exp2(s_scr[ui % 2, soff:soff + n, :] - col_max[ui])
        e_scr[ui % 4, soff:soff + n, :] = e.astype(BF16)
        l = jnp.sum(e, axis=0, keepdims=True)
        col_sum[ui] = l if ci == 0 else col_sum[ui] + l

    def values(pi, ci):
        qt, hd, _ = units[2 * pi]
        rows = slice(hd * DIFF_V_DIM, (hd + 1) * DIFF_V_DIM)
        _, vt_ref, off, soff, n = chunks[ci]
        if ci == 0:
            l1, l2 = col_sum.pop(2 * pi), col_sum.pop(2 * pi + 1)
            norm[pi] = ((lam * l1 / l2).astype(BF16), l1)
        p = e_scr[(2 * pi) % 4, soff:soff + n, :] - e_scr[(2 * pi + 1) % 4, soff:soff + n, :] * norm[pi][0]
        d = _dot(vt_ref[0, rows, off:off + n], p)
        acc[pi] = d if ci == 0 else acc[pi] + d
        if ci == len(chunks) - 1:
            o = acc.pop(pi) / norm.pop(pi)[1]
            o = o * lax.rsqrt(jnp.mean(o * o, axis=0, keepdims=True) + LN_EPS) * out_gain
            o_ref[0, qt * tq:(qt + 1) * tq, rows] = o.T.astype(BF16)

    for r in range(len(units) + 2):
        for ci in range(len(chunks)):
            if r < len(units):
                scores(r, ci)
            if 1 <= r <= len(units):
                exponentials(r - 1, ci)
            if r >= 3 and r % 2 == 1:
                values((r - 3) // 2, ci)


def _attention(q, k_lat, vt_lat, k_ctx, vt_ctx, lq1, lk1, lq2, lk2, subln_g, *, lam_init):
    b, t, w = q.shape
    s_ctx = k_ctx.shape[1]
    tq = min(ATTN_Q_TILE, t)
    tstep = min(ATTN_Q_TILES_PER_STEP * tq, t)
    assert t % ATTN_KEY_CHUNK == 0 and t % tstep == 0
    vec = lambda a: a.reshape(1, -1).astype(F32)
    lam_spec = pl.BlockSpec((1, DIFF_HEAD_DIM), lambda bi, qi: (0, 0))
    return pl.pallas_call(
        functools.partial(_attn_kernel, lam_init=lam_init),
        grid=(b, t // tstep),
        in_specs=[
            lam_spec, lam_spec, lam_spec, lam_spec,
            pl.BlockSpec((DIFF_V_DIM, 1), lambda bi, qi: (0, 0)),
            pl.BlockSpec((1, tstep, w), lambda bi, qi: (bi, qi, 0)),
            pl.BlockSpec((1, t, w), lambda bi, qi: (bi, 0, 0)),
            pl.BlockSpec((1, w, t), lambda bi, qi: (bi, 0, 0)),
            pl.BlockSpec((1, s_ctx, w), lambda bi, qi: (bi, 0, 0)),
            pl.BlockSpec((1, w, s_ctx), lambda bi, qi: (bi, 0, 0)),
        ],
        out_specs=pl.BlockSpec((1, tstep, w), lambda bi, qi: (bi, qi, 0)),
        out_shape=jax.ShapeDtypeStruct((b, t, w), BF16),
        scratch_shapes=[pltpu.VMEM((2, t + s_ctx, tq), F32), pltpu.VMEM((4, t + s_ctx, tq), BF16)],
        compiler_params=_params("parallel", "arbitrary"),
        name="diff_attn",
    )(vec(lq1), vec(lk1), vec(lq2), vec(lk2), subln_g.reshape(DIFF_V_DIM, 1).astype(F32),
      q, k_lat, vt_lat, k_ctx, vt_ctx)


def _mix_out_kernel(h_ref, mod_ref, lng_ref, lnb_ref, o_ref, u_ref, up_ref, un_ref, wpool_ref, pscale_ref, wout_ref,
                    out_ref, ext_ref, *, seq, alpha):
    tm = h_ref.shape[0]
    tpb = seq // tm
    ib = pl.program_id(0) % tpb
    ext_ref[0:POOL_HALO, :] = jnp.where(ib > 0, up_ref[...], 0.0)
    ext_ref[POOL_HALO:POOL_HALO + tm, :] = u_ref[...]
    ext_ref[POOL_HALO + tm:, :] = jnp.where(ib < tpb - 1, un_ref[...], 0.0)
    pos = ib * tm + lax.broadcasted_iota(jnp.int32, (tm, 1), 0)

    y = _dot(o_ref[...], wout_ref[:DIFF_WIDTH, :])
    for g, win in enumerate(POOL_WINDOWS):
        sl = slice(g * POOL_GROUP_DIM, (g + 1) * POOL_GROUP_DIM)
        acc = ext_ref[POOL_HALO - win // 2:POOL_HALO - win // 2 + tm, sl]
        for d in range(1 - win // 2, win // 2):
            acc = acc + ext_ref[POOL_HALO + d:POOL_HALO + d + tm, sl]
        cnt = jnp.minimum(pos + win // 2, seq) - jnp.maximum(pos - win // 2, 0)
        pooled = acc / cnt.astype(F32) - u_ref[:, sl]
        yg = _dot(pooled.astype(BF16), wpool_ref[g]) * pscale_ref[:, sl]
        y = y + _dot(yg.astype(BF16), wout_ref[DIFF_WIDTH + g * POOL_GROUP_DIM:DIFF_WIDTH + (g + 1) * POOL_GROUP_DIM, :])
    r = alpha * h_ref[...] + mod_ref[0, 5:6, :] * y
    out_ref[...] = _layer_norm(r, lng_ref[...], lnb_ref[...])


def _mix_out(h, mod, ln_g, ln_b, o, u, w_pool, pool_scale, w_out, *, seq, alpha):
    n, d = h.shape
    tm = min(ROW_TILE, seq)
    hb = tm // POOL_HALO
    last = n // POOL_HALO - 1
    row_spec = lambda width: pl.BlockSpec((tm, width), lambda i: (i, 0))
    return pl.pallas_call(
        functools.partial(_mix_out_kernel, seq=seq, alpha=alpha),
        grid=(n // tm,),
        in_specs=[
            row_spec(d), _mod_spec(mod, seq // tm), _const_spec((1, d)), _const_spec((1, d)),
            row_spec(DIFF_WIDTH), row_spec(POOL_WIDTH),
            pl.BlockSpec((POOL_HALO, POOL_WIDTH), lambda i: (jnp.maximum(i * hb - 1, 0), 0)),
            pl.BlockSpec((POOL_HALO, POOL_WIDTH), lambda i: (jnp.minimum((i + 1) * hb, last), 0)),
            _const_spec(w_pool.shape), _const_spec((1, POOL_WIDTH)), _const_spec(w_out.shape),
        ],
        out_specs=row_spec(d),
        out_shape=jax.ShapeDtypeStruct((n, d), F32),
        scratch_shapes=[pltpu.VMEM((tm + 2 * POOL_HALO, POOL_WIDTH), F32)],
        compiler_params=_params("parallel"),
        name="mix_out",
    )(h, mod, ln_g.reshape(1, d), ln_b.reshape(1, d), o, u, u, u, w_pool, pool_scale.reshape(1, -1), w_out)


def _conv_kernel(h_ref, hp_ref, hn_ref, mod_ref, lng_ref, lnb_ref, wc1_ref, bc1_ref, wdw_ref, bdw_ref,
                 cg_ref, cb_ref, wc2_ref, bc2_ref, out_ref, x_ref, z_ref, acc_ref, *, seq, alpha):
    tm, d = h_ref.shape
    tpb = seq // tm
    ib = pl.program_id(0) % tpb
    rows = tm + 2 * CONV_HALO

    x_ref[0:CONV_HALO, :] = _modulate(hp_ref[...], mod_ref, 3).astype(BF16)
    x_ref[CONV_HALO:CONV_HALO + tm, :] = _modulate(h_ref[...], mod_ref, 3).astype(BF16)
    x_ref[CONV_HALO + tm:, :] = _modulate(hn_ref[...], mod_ref, 3).astype(BF16)
    ridx = lax.broadcasted_iota(jnp.int32, (rows, 1), 0)
    inside = jnp.logical_and(jnp.logical_or(ridx >= CONV_HALO, ib > 0),
                             jnp.logical_or(ridx < CONV_HALO + tm, ib < tpb - 1))

    first = CONV_HALO - CONV_WIDTH // 2
    span = CONV_ROW_CHUNK + SUBLANES

    def glu_block(j):
        ca = slice(j * CONV_CH_BLOCK, (j + 1) * CONV_CH_BLOCK)
        cg = slice(d + j * CONV_CH_BLOCK, d + (j + 1) * CONV_CH_BLOCK)
        x = x_ref[...]
        a = _dot(x, wc1_ref[:, ca]) + bc1_ref[:, ca]
        g = _dot(x, wc1_ref[:, cg]) + bc1_ref[:, cg]
        z_ref[:, ca] = jnp.where(inside, a * jax.nn.sigmoid(g), 0.0)

    def conv_block(j):
        for c in range(j * CONV_CH_BLOCK // LANES, (j + 1) * CONV_CH_BLOCK // LANES):
            cols = slice(c * LANES, (c + 1) * LANES)
            for r0 in range(0, tm, CONV_ROW_CHUNK):
                acc = jnp.broadcast_to(bdw_ref[:, cols], (CONV_ROW_CHUNK, LANES))
                for b in range(SUBLANES):
                    part = None
                    for a in range(-(-(first + CONV_WIDTH) // SUBLANES)):
                        k = SUBLANES * a + b - first
                        if 0 <= k < CONV_WIDTH:
                            term = z_ref[r0 + SUBLANES * a:r0 + SUBLANES * a + span, cols] * wdw_ref[k:k + 1, cols]
                            part = term if part is None else part + term
                    acc = acc + part[b:b + CONV_ROW_CHUNK]
                acc_ref[r0:r0 + CONV_ROW_CHUNK, cols] = acc

    n_blocks = d // CONV_CH_BLOCK
    glu_block(0)
    for j in range(n_blocks):
        if j + 1 < n_blocks:
            glu_block(j + 1)
        conv_block(j)

    zc = jax.nn.silu(_layer_norm(acc_ref[...], cg_ref[...], cb_ref[...])).astype(BF16)
    y = _dot(zc, wc2_ref[...]) + bc2_ref[...]
    r = alpha * h_ref[...] + mod_ref[0, 5:6, :] * y
    out_ref[...] = _layer_norm(r, lng_ref[...], lnb_ref[...])


def _conv(h, mod, ln_g, ln_b, w_c1, b_c1, w_dw, b_dw, cg, cb, w_c2, b_c2, *, seq, alpha):
    n, d = h.shape
    tm = min(ROW_TILE, seq)
    hb = tm // CONV_HALO
    last = n // CONV_HALO - 1
    row = lambda a: a.reshape(1, -1)
    return pl.pallas_call(
        functools.partial(_conv_kernel, seq=seq, alpha=alpha),
        grid=(n // tm,),
        in_specs=[
            pl.BlockSpec((tm, d), lambda i: (i, 0)),
            pl.BlockSpec((CONV_HALO, d), lambda i: (jnp.maximum(i * hb - 1, 0), 0)),
            pl.BlockSpec((CONV_HALO, d), lambda i: (jnp.minimum((i + 1) * hb, last), 0)),
            _mod_spec(mod, seq // tm), _const_spec((1, d)), _const_spec((1, d)),
            _const_spec(w_c1.shape), _const_spec((1, 2 * d)), _const_spec(w_dw.shape), _const_spec((1, d)),
            _const_spec((1, d)), _const_spec((1, d)), _const_spec(w_c2.shape), _const_spec((1, d)),
        ],
        out_specs=pl.BlockSpec((tm, d), lambda i: (i, 0)),
        out_shape=jax.ShapeDtypeStruct((n, d), F32),
        scratch_shapes=[pltpu.VMEM((tm + 2 * CONV_HALO, d), BF16), pltpu.VMEM((tm + 2 * CONV_HALO, d), F32),
                        pltpu.VMEM((tm, d), F32)],
        compiler_params=_params("parallel"),
        name="conv",
    )(h, h, h, mod, row(ln_g), row(ln_b), w_c1, row(b_c1), w_dw, row(b_dw), row(cg), row(cb), w_c2, row(b_c2))


def kernel(x, c, ctx, c_ctx, w_ada, b_ada, ln_g, ln_b, w_ffn_in, w_ffn_out, w_mix_in, w_mix_out,
           lam_q1, lam_k1, lam_q2, lam_k2, subln_g, w_pool, pool_scale,
           w_c1, b_c1, w_dw, b_dw, conv_ln_g, conv_ln_b, w_c2, b_c2):
    batch, seq, d = x.shape
    ctx_len = ctx.shape[1]
    depth = w_ada.shape[0]
    alpha = (2.0 * depth) ** 0.25
    assert d == D_MODEL and seq % GRID_W == 0
    assert seq % min(ROW_TILE, seq) == 0 and ctx_len % min(ROW_TILE, ctx_len) == 0 and seq % min(ATTN_Q_TILE, seq) == 0
    assert seq % min(FFN_ROW_TILE, seq) == 0 and ctx_len % min(FFN_ROW_TILE, ctx_len) == 0

    n_rows = -(-(batch + 1) // SUBLANES) * SUBLANES
    cc = jnp.zeros((n_rows, d), F32).at[:batch].set(c).at[batch].set(c_ctx)
    mods = _ada(cc, w_ada, b_ada)
    cos, sin = _rope_tables(seq)

    bf = lambda w: w.astype(BF16)
    w_ffn_in, w_ffn_out, w_mix_in, w_mix_out, w_pool = bf(w_ffn_in), bf(w_ffn_out), bf(w_mix_in), bf(w_mix_out), bf(w_pool)
    w_c1, w_c2 = bf(w_c1), bf(w_c2)

    h_lat = x.reshape(batch * seq, d)
    h_ctx = ctx.reshape(batch * ctx_len, d)
    for l in range(depth):
        even = l % 2 == 0
        ctx_later = any(j % 2 == 0 for j in range(l + 1, depth))
        m_lat = mods[l, :batch].reshape(batch, N_MOD, d)
        m_ctx = mods[l, batch].reshape(1, N_MOD, d)
        ffn = lambda h, m, sub, j, rpb: _ffn(h, m, ln_g[l, sub], ln_b[l, sub], w_ffn_in, w_ffn_out,
                                             layer=l, which=j, sub=sub, rows_per_batch=rpb, alpha=alpha)

        h_lat = ffn(h_lat, m_lat, 0, 0, seq)
        if even or ctx_later:
            h_ctx = ffn(h_ctx, m_ctx, 0, 0, ctx_len)

        if even:
            e = l // 2
            lam_init = 0.8 - 0.6 * math.exp(-0.3 * l)
            q, k_lat, vt_lat, u = _mix_in(h_lat, m_lat, w_mix_in[e], cos, sin, seq=seq)
            if ctx_later:
                raise NotImplementedError("context outputs of an even mixer are not needed at this depth")
            k_ctx, vt_ctx = _ctx_kv(h_ctx, m_ctx, w_mix_in[e], seq=ctx_len)
            lat3 = lambda a: a.reshape(batch, seq, DIFF_WIDTH)
            o = _attention(lat3(q), lat3(k_lat), vt_lat, k_ctx.reshape(batch, ctx_len, DIFF_WIDTH), vt_ctx,
                           lam_q1[e], lam_k1[e], lam_q2[e], lam_k2[e], subln_g[e], lam_init=lam_init)
            h_lat = _mix_out(h_lat, m_lat, ln_g[l, 1], ln_b[l, 1], o.reshape(batch * seq, DIFF_WIDTH), u,
                             w_pool[e], pool_scale[e], w_mix_out[e], seq=seq, alpha=alpha)
        else:
            o = l // 2
            if ctx_later:
                raise NotImplementedError("context outputs of an odd mixer are not needed at this depth")
            h_lat = _conv(h_lat, m_lat, ln_g[l, 1], ln_b[l, 1], w_c1[o], b_c1[o], w_dw[o], b_dw[o],
                          conv_ln_g[o], conv_ln_b[o], w_c2[o], b_c2[o], seq=seq, alpha=alpha)

        h_lat = ffn(h_lat, m_lat, 2, 1, seq)
        if ctx_later:
            h_ctx = ffn(h_ctx, m_ctx, 2, 1, ctx_len)
    return h_lat.reshape(batch, seq, d)
```
